```python
import jax, jax.numpy as jnp
from jax import lax
import numpy as np

D_MODEL = 1024
BATCH = 4
SEQ = 8192
DEPTH = 4

HEAD_DIM = 64
D_RNN = D_MODEL // 2
H_RNN = D_RNN // HEAD_DIM
RNN_BLOCK = D_RNN // H_RNN
CONV_W = 4
RG_C = 8.0
H_FOX = (D_MODEL // 2) // HEAD_DIM
FOX_W = H_FOX * HEAD_DIM
H_SB = D_MODEL // HEAD_DIM
SB_W = H_SB * HEAD_DIM
D_FF = 11 * D_MODEL // 4
Q_BLOCK = 128
RMS_EPS = 1e-6
N_EVEN = (DEPTH + 1) // 2
N_ODD = DEPTH // 2
HY_IN = 2 * D_RNN + 3 * FOX_W + H_FOX
HY_MIX = D_RNN + FOX_W

kernel_name = "hybrid_rglru_fox_stickbreak_macaron"


def _rmsnorm(x, g):
    x32 = x.astype(jnp.float32)
    y = x32 * lax.rsqrt(jnp.mean(x32 * x32, axis=-1, keepdims=True) + RMS_EPS)
    return (y * g.astype(jnp.float32)).astype(x.dtype)


def _swiglu(h, w_in, w_out):
    a, b = jnp.split(h @ w_in, 2, axis=-1)
    return (jax.nn.silu(a) * b) @ w_out


def _lin_combine(left, right):
    a1, b1 = left
    a2, b2 = right
    return a1 * a2, a2 * b1 + b2


def _rglru_branch(u, gate, conv_w, conv_b, gate_w, gate_b, lam):
    B, S, _ = u.shape
    up = jnp.pad(u, ((0, 0), (CONV_W - 1, 0), (0, 0)))
    xc = conv_b + sum(conv_w[j] * up[:, j:j + S] for j in range(CONV_W))
    xh = xc.reshape(B, S, H_RNN, RNN_BLOCK)
    g = jnp.einsum('bshi,ghij->gbshj', xh, gate_w).reshape(2, B, S, D_RNN) + gate_b[:, None, None, :]
    r = jax.nn.sigmoid(g[0].astype(jnp.float32))
    i = jax.nn.sigmoid(g[1].astype(jnp.float32))
    log_a = -RG_C * r * jax.nn.softplus(-lam.astype(jnp.float32))
    a = jnp.exp(log_a)
    b = jnp.sqrt(-jnp.expm1(2.0 * log_a)) * (i * xc.astype(jnp.float32))
    _, h = lax.associative_scan(_lin_combine, (a, b), axis=1)
    return h.astype(u.dtype) * jax.nn.gelu(gate)


def _to_query_blocks(q):
    B, H, S, d = q.shape
    return q.reshape(B, H, S // Q_BLOCK, Q_BLOCK, d).transpose(2, 0, 1, 3, 4)


def _from_query_blocks(o):
    nb, B, H, QB, d = o.shape
    return o.transpose(1, 0, 3, 2, 4).reshape(B, nb * QB, H * d)


def _fox_attention(q, k, v, logf):
    B, H, S, d = q.shape
    c = jnp.cumsum(logf, axis=-1)
    nb = S // Q_BLOCK
    qb = _to_query_blocks(q)
    cb = c.reshape(B, H, nb, Q_BLOCK).transpose(2, 0, 1, 3)
    kpos = jnp.arange(S)
    scale = d ** -0.5

    def block(args):
        qi, ci, bi = args
        qpos = bi * Q_BLOCK + jnp.arange(Q_BLOCK)
        s = jnp.einsum('bhqd,bhkd->bhqk', qi, k).astype(jnp.float32) * scale
        s = s + ci[..., None] - c[:, :, None, :]
        s = jnp.where(kpos[None, :] <= qpos[:, None], s, -jnp.inf)
        p = jax.nn.softmax(s, axis=-1)
        return jnp.einsum('bhqk,bhkd->bhqd', p.astype(v.dtype), v)

    return _from_query_blocks(lax.map(block, (qb, cb, jnp.arange(nb))))


def _stick_breaking_attention(q, k, v):
    B, H, S, d = q.shape
    nb = S // Q_BLOCK
    qb = _to_query_blocks(q)
    kpos = jnp.arange(S)
    scale = d ** -0.5

    def block(args):
        qi, bi = args
        qpos = bi * Q_BLOCK + jnp.arange(Q_BLOCK)
        mask = kpos[None, :] < qpos[:, None]
        z = jnp.einsum('bhqd,bhkd->bhqk', qi, k).astype(jnp.float32) * scale
        log_nb = jnp.where(mask, jax.nn.log_sigmoid(-z), 0.0)
        after = lax.cumsum(log_nb, axis=3, reverse=True) - log_nb
        w = jnp.where(mask, jnp.exp(jax.nn.log_sigmoid(z) + after), 0.0)
        return jnp.einsum('bhqk,bhkd->bhqd', w.astype(v.dtype), v)

    return _from_query_blocks(lax.map(block, (qb, jnp.arange(nb))))


def _heads(t, n):
    B, S, _ = t.shape
    return t.reshape(B, S, n, HEAD_DIM)


def _hybrid_mixer(h, w_in, conv_w, conv_b, gate_w, gate_b, lam, f_b, qk_g, w_out):
    proj = h @ w_in
    o = 2 * D_RNN
    u, gate, q, k, v, f = jnp.split(proj, [D_RNN, o, o + FOX_W, o + 2 * FOX_W, o + 3 * FOX_W], axis=-1)
    y_rnn = _rglru_branch(u, gate, conv_w, conv_b, gate_w, gate_b, lam)
    qh = _rmsnorm(_heads(q, H_FOX), qk_g[0]).transpose(0, 2, 1, 3)
    kh = _rmsnorm(_heads(k, H_FOX), qk_g[1]).transpose(0, 2, 1, 3)
    vh = _heads(v, H_FOX).transpose(0, 2, 1, 3)
    logf = jax.nn.log_sigmoid((f + f_b).astype(jnp.float32)).transpose(0, 2, 1)
    y_fox = _fox_attention(qh, kh, vh, logf)
    return jnp.concatenate([y_rnn, y_fox], axis=-1) @ w_out


def _sb_mixer(h, w_qkv, w_out):
    q, k, v = jnp.split(h @ w_qkv, 3, axis=-1)
    qh, kh, vh = (_heads(t, H_SB).transpose(0, 2, 1, 3) for t in (q, k, v))
    return _stick_breaking_attention(qh, kh, vh) @ w_out


def setup_inputs(seed: int = 0) -> dict:
    key = jax.random.key(seed)
    ks = jax.random.split(key, 20)
    f32 = jnp.float32

    def nrm(k, shape, fan_in):
        return jax.random.normal(k, shape, f32) * (fan_in ** -0.5)

    a0 = jax.random.uniform(ks[9], (N_EVEN, D_RNN), f32, 0.9, 0.999)
    return {
        "x": jax.random.normal(ks[0], (BATCH, SEQ, D_MODEL), f32),
        "ffn_norm": 1.0 + 0.05 * jax.random.normal(ks[1], (DEPTH, 2, D_MODEL), f32),
        "ffn_w_in": nrm(ks[2], (DEPTH, 2, D_MODEL, 2 * D_FF), D_MODEL),
        "ffn_w_out": nrm(ks[3], (DEPTH, 2, D_FF, D_MODEL), D_FF),
        "mix_norm": 1.0 + 0.05 * jax.random.normal(ks[4], (DEPTH, D_MODEL), f32),
        "hy_w_in": nrm(ks[5], (N_EVEN, D_MODEL, HY_IN), D_MODEL),
        "rg_conv_w": nrm(ks[6], (N_EVEN, CONV_W, D_RNN), CONV_W),
        "rg_conv_b": 0.02 * jax.random.normal(ks[7], (N_EVEN, D_RNN), f32),
        "rg_gate_w": nrm(ks[8], (N_EVEN, 2, H_RNN, RNN_BLOCK, RNN_BLOCK), RNN_BLOCK),
        "rg_gate_b": 0.02 * jax.random.normal(ks[10], (N_EVEN, 2, D_RNN), f32),
        "rg_lambda": jnp.log(a0) - jnp.log1p(-a0),
        "fox_fgate_b": jax.random.uniform(ks[11], (N_EVEN, H_FOX), f32, 1.0, 4.0),
        "fox_qk_norm": 1.0 + 0.05 * jax.random.normal(ks[12], (N_EVEN, 2, HEAD_DIM), f32),
        "hy_w_out": nrm(ks[13], (N_EVEN, HY_MIX, D_MODEL), HY_MIX),
        "sb_w_qkv": nrm(ks[14], (N_ODD, D_MODEL, 3 * SB_W), D_MODEL),
        "sb_w_out": nrm(ks[15], (N_ODD, SB_W, D_MODEL), SB_W),
    }


def reference(x, ffn_norm, ffn_w_in, ffn_w_out, mix_norm, hy_w_in, rg_conv_w, rg_conv_b,
              rg_gate_w, rg_gate_b, rg_lambda, fox_fgate_b, fox_qk_norm, hy_w_out,
              sb_w_qkv, sb_w_out):
    for layer in range(DEPTH):
        x = x + 0.5 * _swiglu(_rmsnorm(x, ffn_norm[layer, 0]), ffn_w_in[layer, 0], ffn_w_out[layer, 0])
        h = _rmsnorm(x, mix_norm[layer])
        if layer % 2 == 0:
            e = layer // 2
            y = _hybrid_mixer(h, hy_w_in[e], rg_conv_w[e], rg_conv_b[e], rg_gate_w[e], rg_gate_b[e],
                              rg_lambda[e], fox_fgate_b[e], fox_qk_norm[e], hy_w_out[e])
        else:
            o = layer // 2
            y = _sb_mixer(h, sb_w_qkv[o], sb_w_out[o])
        x = x + y
        x = x + 0.5 * _swiglu(_rmsnorm(x, ffn_norm[layer, 1]), ffn_w_in[layer, 1], ffn_w_out[layer, 1])
    return x
```

```python
import functools

import jax
import jax.numpy as jnp
from jax import lax
from jax.experimental import pallas as pl
from jax.experimental.pallas import tpu as pltpu

F32 = jnp.float32
BF16 = jnp.bfloat16

HEAD_DIM = 64
RMS_EPS = 1e-6
RG_C = 8.0
CONV_W = 4

V7X_LANES = 128
V7X_SUBLANES = 8
V7X_VMEM_LIMIT_BYTES = 56 * 1024 * 1024
HEADS_PER_TILE = V7X_LANES // HEAD_DIM

EXP_ZERO_BELOW = -104.0
NEG_BIG = -1e30

TOKEN_TILE = 512
TIME_TILE = 512
ATTN_TILE = 256


def _params(semantics):
    return pltpu.CompilerParams(dimension_semantics=semantics,
                                vmem_limit_bytes=V7X_VMEM_LIMIT_BYTES)


def _resident(shape):
    nd = len(shape)
    return pl.BlockSpec(shape, lambda *_: (0,) * nd, pipeline_mode=pl.Buffered(1))


def _rmsnorm_rows(x, g):
    ms = jnp.mean(x * x, axis=-1, keepdims=True)
    return x * lax.rsqrt(ms + RMS_EPS) * g


def _softplus(x):
    return jnp.maximum(x, 0.0) + jnp.log(1.0 + jnp.exp(-jnp.abs(x)))


def _sigmoid(x):
    return 1.0 / (1.0 + jnp.exp(-x))


def _ffn_kernel(x_ref, g_ref, win_ref, wout_ref, o_ref, *, d_ff):
    x = x_ref[...]
    h = _rmsnorm_rows(x, g_ref[...]).astype(BF16)
    ab = jnp.dot(h, win_ref[...], preferred_element_type=F32)
    a = ab[:, :d_ff]
    b = ab[:, d_ff:]
    t = (a * _sigmoid(a) * b).astype(BF16)
    y = jnp.dot(t, wout_ref[...], preferred_element_type=F32)
    o_ref[...] = x + 0.5 * y


def _ffn(x2, g, w_in, w_out):
    t, d = x2.shape
    d_ff = w_out.shape[0]
    tm = min(TOKEN_TILE, t)
    return pl.pallas_call(
        functools.partial(_ffn_kernel, d_ff=d_ff),
        out_shape=jax.ShapeDtypeStruct((t, d), F32),
        grid=(t // tm,),
        in_specs=[pl.BlockSpec((tm, d), lambda i: (i, 0)),
                  _resident((1, d)),
                  _resident(w_in.shape),
                  _resident(w_out.shape)],
        out_specs=pl.BlockSpec((tm, d), lambda i: (i, 0)),
        compiler_params=_params(("parallel",)),
        name="ffn",
    )(x2, g.reshape(1, d), w_in, w_out)


def _outproj_kernel(*refs, n_parts):
    x_ref = refs[0]
    o_ref = refs[-1]
    acc = x_ref[...]
    for i in range(n_parts):
        acc = acc + jnp.dot(refs[1 + 2 * i][...], refs[2 + 2 * i][...],
                            preferred_element_type=F32)
    o_ref[...] = acc


def _outproj(x2, parts):
    t, d = x2.shape
    tm = min(TOKEN_TILE, t)
    in_specs = [pl.BlockSpec((tm, d), lambda i: (i, 0))]
    args = [x2]
    for y, w in parts:
        in_specs.append(pl.BlockSpec((tm, y.shape[1]), lambda i: (i, 0)))
        in_specs.append(_resident(w.shape))
        args += [y, w]
    return pl.pallas_call(
        functools.partial(_outproj_kernel, n_parts=len(parts)),
        out_shape=jax.ShapeDtypeStruct((t, d), F32),
        grid=(t // tm,),
        in_specs=in_specs,
        out_specs=pl.BlockSpec((tm, d), lambda i: (i, 0)),
        compiler_params=_params(("parallel",)),
        name="outproj",
    )(*args)


def _head_rmsnorm(t, gain_row, first_head):
    outs = []
    for j in range(t.shape[1] // V7X_LANES):
        tj = t[:, j * V7X_LANES:(j + 1) * V7X_LANES]
        sq = tj * tj
        s0 = jnp.sum(jnp.where(first_head, sq, 0.0), axis=-1, keepdims=True)
        s1 = jnp.sum(jnp.where(first_head, 0.0, sq), axis=-1, keepdims=True)
        ms = jnp.where(first_head, s0, s1) * (1.0 / HEAD_DIM)
        outs.append(tj * lax.rsqrt(ms + RMS_EPS) * gain_row)
    return jnp.concatenate(outs, axis=-1)


def _hyin_kernel(x_ref, g_ref, w_ref, qkg_ref, fb_ref,
                 ug_ref, q_ref, k_ref, v_ref, lf_ref, *, d_rnn, fox_w):
    x = x_ref[...]
    h = _rmsnorm_rows(x, g_ref[...]).astype(BF16)
    p = jnp.dot(h, w_ref[...], preferred_element_type=F32)
    o = 2 * d_rnn
    ug_ref[...] = p[:, :o]
    first_head = lax.broadcasted_iota(jnp.int32, (x.shape[0], V7X_LANES), 1) < HEAD_DIM
    scale = HEAD_DIM ** -0.5
    q = _head_rmsnorm(p[:, o:o + fox_w], qkg_ref[0:1, :], first_head)
    q_ref[...] = (q * scale).astype(BF16)
    k = _head_rmsnorm(p[:, o + fox_w:o + 2 * fox_w], qkg_ref[1:2, :], first_head)
    k_ref[...] = k.astype(BF16)
    v_ref[...] = p[:, o + 2 * fox_w:o + 3 * fox_w].astype(BF16)
    f = p[:, o + 3 * fox_w:] + fb_ref[...]
    lf_ref[...] = -_softplus(-f)


def _hyin(x2, g, w_pad, qk_gain, fb_pad, d_rnn, fox_w):
    t, d = x2.shape
    tm = min(TOKEN_TILE, t)
    row = lambda i: (i, 0)
    return pl.pallas_call(
        functools.partial(_hyin_kernel, d_rnn=d_rnn, fox_w=fox_w),
        out_shape=(jax.ShapeDtypeStruct((t, 2 * d_rnn), F32),
                   jax.ShapeDtypeStruct((t, fox_w), BF16),
                   jax.ShapeDtypeStruct((t, fox_w), BF16),
                   jax.ShapeDtypeStruct((t, fox_w), BF16),
                   jax.ShapeDtypeStruct((t, V7X_LANES), F32)),
        grid=(t // tm,),
        in_specs=[pl.BlockSpec((tm, d), row),
                  _resident((1, d)),
                  _resident(w_pad.shape),
                  _resident(qk_gain.shape),
                  _resident(fb_pad.shape)],
        out_specs=(pl.BlockSpec((tm, 2 * d_rnn), row),
                   pl.BlockSpec((tm, fox_w), row),
                   pl.BlockSpec((tm, fox_w), row),
                   pl.BlockSpec((tm, fox_w), row),
                   pl.BlockSpec((tm, V7X_LANES), row)),
        compiler_params=_params(("parallel",)),
        name="hyin",
    )(x2, g.reshape(1, d), w_pad, qk_gain, fb_pad)


def _scan_linear(a, b, row):
    n = a.shape[0]
    s = 1
    while s < n:
        keep = row >= s
        b = jnp.where(keep, a * pltpu.roll(b, s, 0) + b, b)
        a = jnp.where(keep, a * pltpu.roll(a, s, 0), a)
        s *= 2
    return a, b


def _cumsum_rows(x, row):
    n = x.shape[0]
    s = 1
    while s < n:
        x = jnp.where(row >= s, x + pltpu.roll(x, s, 0), x)
        s *= 2
    return x


def _rglru_kernel(ug_ref, lf_ref, cw_ref, cb_ref, gw_ref, gb_ref, lam_ref,
                  y_ref, c_ref, ct_ref, tail_ref, hc_ref, cc_ref, *, d_rnn, n_heads):
    @pl.when(pl.program_id(1) == 0)
    def _():
        tail_ref[...] = jnp.zeros_like(tail_ref)
        hc_ref[...] = jnp.zeros_like(hc_ref)
        cc_ref[...] = jnp.zeros_like(cc_ref)

    u = ug_ref[0, :, :d_rnn]
    gate = ug_ref[0, :, d_rnn:]
    tt = u.shape[0]
    row = lax.broadcasted_iota(jnp.int32, (tt, d_rnn), 0)

    tail = tail_ref[...]
    head_row = lax.broadcasted_iota(jnp.int32, (V7X_SUBLANES, d_rnn), 0)
    xc = cb_ref[...] + cw_ref[CONV_W - 1:CONV_W, :] * u
    for k in range(1, CONV_W):
        r = pltpu.roll(u, k, 0)
        head = jnp.where(head_row < k, pltpu.roll(tail, k, 0), r[:V7X_SUBLANES])
        shifted = jnp.concatenate([head, r[V7X_SUBLANES:]], axis=0)
        xc = xc + cw_ref[CONV_W - 1 - k:CONV_W - k, :] * shifted
    tail_ref[...] = u[tt - V7X_SUBLANES:, :]

    g = jnp.dot(xc.astype(BF16), gw_ref[...], preferred_element_type=F32) + gb_ref[...]
    r_gate = _sigmoid(g[:, :d_rnn])
    i_gate = _sigmoid(g[:, d_rnn:])
    log_a = -RG_C * r_gate * _softplus(-lam_ref[...])
    a = jnp.exp(log_a)
    one_minus_a2 = -jnp.tanh(log_a) * (a * a + 1.0)
    b = jnp.sqrt(one_minus_a2) * (i_gate * xc)

    big_a, big_b = _scan_linear(a, b, row)
    hseq = big_a * hc_ref[...] + big_b
    hc_ref[...] = hseq[tt - 1:tt, :]
    y_ref[0] = (hseq * jax.nn.gelu(gate)).astype(BF16)

    lrow = lax.broadcasted_iota(jnp.int32, (tt, V7X_LANES), 0)
    c = _cumsum_rows(lf_ref[0], lrow) + cc_ref[...]
    cc_ref[...] = c[tt - 1:tt, :]
    c_ref[0] = c
    ct_ref[0] = c.T[:n_heads, :]


def _rglru(ug, lf, conv_w, conv_b, gate_w_dense, gate_b, lam, n_heads):
    bsz, s, two_d = ug.shape
    d_rnn = two_d // 2
    tt = min(TIME_TILE, s)
    blk = lambda b, i: (b, i, 0)
    return pl.pallas_call(
        functools.partial(_rglru_kernel, d_rnn=d_rnn, n_heads=n_heads),
        out_shape=(jax.ShapeDtypeStruct((bsz, s, d_rnn), BF16),
                   jax.ShapeDtypeStruct((bsz, s, V7X_LANES), F32),
                   jax.ShapeDtypeStruct((bsz, n_heads, s), F32)),
        grid=(bsz, s // tt),
        in_specs=[pl.BlockSpec((1, tt, two_d), blk),
                  pl.BlockSpec((1, tt, V7X_LANES), blk),
                  _resident(conv_w.shape),
                  _resident((1, d_rnn)),
                  _resident(gate_w_dense.shape),
                  _resident((1, 2 * d_rnn)),
                  _resident((1, d_rnn))],
        out_specs=(pl.BlockSpec((1, tt, d_rnn), blk),
                   pl.BlockSpec((1, tt, V7X_LANES), blk),
                   pl.BlockSpec((1, n_heads, tt), lambda b, i: (b, 0, i))),
        scratch_shapes=[pltpu.VMEM((V7X_SUBLANES, d_rnn), F32),
                        pltpu.VMEM((1, d_rnn), F32),
                        pltpu.VMEM((1, V7X_LANES), F32)],
        compiler_params=_params(("parallel", "arbitrary")),
        name="rglru",
    )(ug, lf, conv_w, conv_b.reshape(1, d_rnn), gate_w_dense,
      gate_b.reshape(1, 2 * d_rnn), lam.reshape(1, d_rnn))


def _qk(qm, kj):
    return lax.dot_general(qm, kj, (((1,), (1,)), ((), ())), preferred_element_type=F32)


def _max_key_norms(k_ref, kmax_ref, chunk):
    s = k_ref.shape[1]
    first = lax.broadcasted_iota(jnp.int32, (chunk, V7X_LANES), 1) < HEAD_DIM
    m0 = jnp.zeros((1, 1), F32)
    m1 = jnp.zeros((1, 1), F32)
    for c in range(s // chunk):
        kk = k_ref[0, c * chunk:(c + 1) * chunk, :].astype(F32)
        sq = kk * kk
        n0 = jnp.sum(jnp.where(first, sq, 0.0), axis=-1, keepdims=True)
        n1 = jnp.sum(jnp.where(first, 0.0, sq), axis=-1, keepdims=True)
        m0 = jnp.maximum(m0, jnp.max(n0, axis=0, keepdims=True))
        m1 = jnp.maximum(m1, jnp.max(n1, axis=0, keepdims=True))
    kmax_ref[...] = jnp.sqrt(jnp.where(first[:1], m0, m1))


def _logit_bound(qm, kmax_row, head_mask_row):
    qf = qm.astype(F32)
    qn = jnp.sqrt(jnp.sum(qf * qf, axis=-1, keepdims=True))
    km = jnp.max(jnp.where(head_mask_row, kmax_row, 0.0), axis=-1, keepdims=True)
    return qn * km * 1.01 + 1e-3


def _sb_kernel(q_ref, k_ref, v_ref, o_ref, kmax_ref):
    qi = pl.program_id(2)
    tq = q_ref.shape[1]
    tk = tq

    @pl.when(qi == 0)
    def _():
        _max_key_norms(k_ref, kmax_ref, min(1024, k_ref.shape[1]))

    lane = lax.broadcasted_iota(jnp.int32, (tq, V7X_LANES), 1)
    rows = lax.broadcasted_iota(jnp.int32, (tq, tk), 0)
    cols = lax.broadcasted_iota(jnp.int32, (tq, tk), 1)
    strictly_past = cols < rows
    tri = jnp.where(rows >= cols, 1.0, 0.0).astype(BF16)
    q = q_ref[0]
    kmax_row = kmax_ref[...]

    def block(qm, j, run, acc, diag):
        start = pl.multiple_of(j * tk, tk)
        kj = k_ref[0, pl.ds(start, tk), :]
        vj = v_ref[0, pl.ds(start, tk), :]
        z = _qk(qm, kj)
        log_nb = -_softplus(z)
        if diag:
            log_nb = jnp.where(strictly_past, log_nb, 0.0)
        hi = log_nb.astype(BF16)
        lo = (log_nb - hi.astype(F32)).astype(BF16)
        incl = (jnp.dot(hi, tri, preferred_element_type=F32)
                + jnp.dot(lo, tri, preferred_element_type=F32))
        w = jnp.exp(z + incl + run)
        if diag:
            w = jnp.where(strictly_past, w, 0.0)
        acc = acc + jnp.dot(w.astype(BF16), vj, preferred_element_type=F32)
        run = run + incl[:, 0:1]
        return run, acc

    outs = []
    for hd in range(HEADS_PER_TILE):
        head_mask = (lane >= hd * HEAD_DIM) & (lane < (hd + 1) * HEAD_DIM)
        qm = jnp.where(head_mask, q, jnp.zeros_like(q))
        bound = _logit_bound(qm, kmax_row, head_mask[:1])
        run, acc = block(qm, qi, jnp.zeros((tq, 1), F32), jnp.zeros((tq, V7X_LANES), F32), True)

        def live(run):
            return jnp.max(run + bound) > EXP_ZERO_BELOW

        def cond(carry):
            j, go, _, _ = carry
            return jnp.logical_and(j >= 0, go)

        def body(carry, qm=qm, live=live):
            j, _, run, acc = carry
            run, acc = block(qm, j, run, acc, False)
            return j - 1, live(run), run, acc

        _, _, _, acc = lax.while_loop(cond, body, (qi - 1, live(run), run, acc))
        outs.append((head_mask, acc))

    res = outs[0][1]
    for head_mask, acc in outs[1:]:
        res = jnp.where(head_mask, acc, res)
    o_ref[0] = res.astype(BF16)


def _sb_attention(q, k, v):
    bsz, s, w = q.shape
    tq = min(ATTN_TILE, s)
    n_tiles = w // V7X_LANES
    return pl.pallas_call(
        _sb_kernel,
        out_shape=jax.ShapeDtypeStruct((bsz, s, w), BF16),
        grid=(bsz, n_tiles, s // tq),
        in_specs=[pl.BlockSpec((1, tq, V7X_LANES), lambda b, p, i: (b, i, p)),
                  pl.BlockSpec((1, s, V7X_LANES), lambda b, p, i: (b, 0, p)),
                  pl.BlockSpec((1, s, V7X_LANES), lambda b, p, i: (b, 0, p))],
        out_specs=pl.BlockSpec((1, tq, V7X_LANES), lambda b, p, i: (b, i, p)),
        scratch_shapes=[pltpu.VMEM((1, V7X_LANES), F32)],
        compiler_params=_params(("parallel", "parallel", "arbitrary")),
        name="sb_attn",
    )(q, k, v)


def _fox_kernel(q_ref, k_ref, v_ref, c_ref, ct_ref, o_ref, kmax_ref):
    tile = pl.program_id(1)
    qi = pl.program_id(2)
    tq = q_ref.shape[1]
    tk = tq

    @pl.when(qi == 0)
    def _():
        _max_key_norms(k_ref, kmax_ref, min(1024, k_ref.shape[1]))

    lane = lax.broadcasted_iota(jnp.int32, (tq, V7X_LANES), 1)
    rows = lax.broadcasted_iota(jnp.int32, (tq, tk), 0)
    cols = lax.broadcasted_iota(jnp.int32, (tq, tk), 1)
    causal = cols <= rows
    q = q_ref[0]
    c_blk = c_ref[0]
    kmax_row = kmax_ref[...]

    def block(qm, head, cq, j, m, l, acc, diag):
        start = pl.multiple_of(j * tk, tk)
        kj = k_ref[0, pl.ds(start, tk), :]
        vj = v_ref[0, pl.ds(start, tk), :]
        ck = ct_ref[0, pl.ds(head, 1), pl.ds(start, tk)]
        s = _qk(qm, kj) + (cq - ck)
        if diag:
            s = jnp.where(causal, s, NEG_BIG)
        m_new = jnp.maximum(m, jnp.max(s, axis=-1, keepdims=True))
        alpha = jnp.exp(m - m_new)
        p = jnp.exp(s - m_new)
        l = alpha * l + jnp.sum(p, axis=-1, keepdims=True)
        acc = alpha * acc + jnp.dot(p.astype(BF16), vj, preferred_element_type=F32)
        reach = cq - jnp.max(ck, axis=-1, keepdims=True)
        return m_new, l, acc, reach

    outs = []
    for hd in range(HEADS_PER_TILE):
        head = tile * HEADS_PER_TILE + hd
        head_mask = (lane >= hd * HEAD_DIM) & (lane < (hd + 1) * HEAD_DIM)
        qm = jnp.where(head_mask, q, jnp.zeros_like(q))
        bound = _logit_bound(qm, kmax_row, head_mask[:1])
        cq = jnp.sum(jnp.where(lane == head, c_blk, 0.0), axis=-1, keepdims=True)
        m, l, acc, reach = block(qm, head, cq, qi,
                                 jnp.full((tq, 1), NEG_BIG, F32), jnp.zeros((tq, 1), F32),
                                 jnp.zeros((tq, V7X_LANES), F32), True)

        def live(m, reach):
            return jnp.max(bound + reach - m) > EXP_ZERO_BELOW

        def cond(carry):
            return jnp.logical_and(carry[0] >= 0, carry[1])

        def body(carry, qm=qm, head=head, cq=cq, live=live):
            j, _, m, l, acc = carry
            m, l, acc, reach = block(qm, head, cq, j, m, l, acc, False)
            return j - 1, live(m, reach), m, l, acc

        _, _, _, l, acc = lax.while_loop(cond, body, (qi - 1, live(m, reach), m, l, acc))
        outs.append((head_mask, acc / l))

    res = outs[0][1]
    for head_mask, val in outs[1:]:
        res = jnp.where(head_mask, val, res)
    o_ref[0] = res.astype(BF16)


def _fox_attention(q, k, v, c, ct):
    bsz, s, w = q.shape
    n_heads = ct.shape[1]
    tq = min(ATTN_TILE, s)
    n_tiles = w // V7X_LANES
    return pl.pallas_call(
        _fox_kernel,
        out_shape=jax.ShapeDtypeStruct((bsz, s, w), BF16),
        grid=(bsz, n_tiles, s // tq),
        in_specs=[pl.BlockSpec((1, tq, V7X_LANES), lambda b, p, i: (b, i, p)),
                  pl.BlockSpec((1, s, V7X_LANES), lambda b, p, i: (b, 0, p)),
                  pl.BlockSpec((1, s, V7X_LANES), lambda b, p, i: (b, 0, p)),
                  pl.BlockSpec((1, tq, V7X_LANES), lambda b, p, i: (b, i, 0)),
                  pl.BlockSpec((1, n_heads, s), lambda b, p, i: (b, 0, 0))],
        out_specs=pl.BlockSpec((1, tq, V7X_LANES), lambda b, p, i: (b, i, p)),
        scratch_shapes=[pltpu.VMEM((1, V7X_LANES), F32)],
        compiler_params=_params(("parallel", "parallel", "arbitrary")),
        name="fox_attn",
    )(q, k, v, c, ct)


def _sbin_kernel(x_ref, g_ref, w_ref, q_ref, k_ref, v_ref, *, width):
    x = x_ref[...]
    h = _rmsnorm_rows(x, g_ref[...]).astype(BF16)
    p = jnp.dot(h, w_ref[...], preferred_element_type=F32)
    q_ref[...] = (p[:, :width] * (HEAD_DIM ** -0.5)).astype(BF16)
    k_ref[...] = p[:, width:2 * width].astype(BF16)
    v_ref[...] = p[:, 2 * width:].astype(BF16)


def _sbin(x2, g, w_qkv):
    t, d = x2.shape
    width = w_qkv.shape[1] // 3
    tm = min(TOKEN_TILE, t)
    row = lambda i: (i, 0)
    return pl.pallas_call(
        functools.partial(_sbin_kernel, width=width),
        out_shape=tuple(jax.ShapeDtypeStruct((t, width), BF16) for _ in range(3)),
        grid=(t // tm,),
        in_specs=[pl.BlockSpec((tm, d), row), _resident((1, d)), _resident(w_qkv.shape)],
        out_specs=tuple(pl.BlockSpec((tm, width), row) for _ in range(3)),
        compiler_params=_params(("parallel",)),
        name="sbin",
    )(x2, g.reshape(1, d), w_qkv)


def _block_diag_gates(gate_w):
    n_gates, n_blocks, r, _ = gate_w.shape
    eye = jnp.eye(n_blocks, dtype=gate_w.dtype)
    dense = jnp.einsum('ghij,hk->ghikj', gate_w, eye).reshape(n_gates, n_blocks * r, n_blocks * r)
    return jnp.concatenate([dense[g] for g in range(n_gates)], axis=1)


def kernel(x, ffn_norm, ffn_w_in, ffn_w_out, mix_norm, hy_w_in, rg_conv_w, rg_conv_b, rg_gate_w,
           rg_gate_b, rg_lambda, fox_fgate_b, fox_qk_norm, hy_w_out, sb_w_qkv, sb_w_out):
    bsz, s, d = x.shape
    depth = ffn_norm.shape[0]
    d_rnn = rg_conv_w.shape[2]
    n_fox = fox_fgate_b.shape[1]
    fox_w = n_fox * HEAD_DIM
    t = bsz * s

    x2 = x.reshape(t, d)
    for layer in range(depth):
        x2 = _ffn(x2, ffn_norm[layer, 0], ffn_w_in[layer, 0].astype(BF16),
                  ffn_w_out[layer, 0].astype(BF16))
        if layer % 2 == 0:
            e = layer // 2
            pad = V7X_LANES - n_fox
            w_pad = jnp.pad(hy_w_in[e], ((0, 0), (0, pad))).astype(BF16)
            fb_pad = jnp.pad(fox_fgate_b[e], (0, pad)).reshape(1, V7X_LANES)
            qk_gain = jnp.tile(fox_qk_norm[e], (1, HEADS_PER_TILE))
            ug, q, k, v, lf = _hyin(x2, mix_norm[layer], w_pad, qk_gain, fb_pad, d_rnn, fox_w)
            y_rnn, c, ct = _rglru(ug.reshape(bsz, s, 2 * d_rnn), lf.reshape(bsz, s, V7X_LANES),
                                  rg_conv_w[e], rg_conv_b[e],
                                  _block_diag_gates(rg_gate_w[e]).astype(BF16),
                                  rg_gate_b[e].reshape(-1), rg_lambda[e], n_fox)
            y_fox = _fox_attention(q.reshape(bsz, s, fox_w), k.reshape(bsz, s, fox_w),
                                   v.reshape(bsz, s, fox_w), c, ct)
            w_out = hy_w_out[e].astype(BF16)
            x2 = _outproj(x2, [(y_rnn.reshape(t, d_rnn), w_out[:d_rnn]),
                               (y_fox.reshape(t, fox_w), w_out[d_rnn:])])
        else:
            o = layer // 2
            q, k, v = _sbin(x2, mix_norm[layer], sb_w_qkv[o].astype(BF16))
            width = q.shape[1]
            y = _sb_attention(q.reshape(bsz, s, width), k.reshape(bsz, s, width),
                              v.reshape(bsz, s, width))
            x2 = _outproj(x2, [(y.reshape(t, width), sb_w_out[o].astype(BF16))])
        x2 = _ffn(x2, ffn_norm[layer, 1], ffn_w_in[layer, 1].astype(BF16),
                  ffn_w_out[layer, 1].astype(BF16))
    return x2.reshape(bsz, s, d)
```

```python
import functools

import jax
import jax.numpy as jnp
from jax import lax
from jax.experimental import pallas as pl
from jax.experimental.pallas import tpu as pltpu

F32 = jnp.float32
BF16 = jnp.bfloat16

HEAD_DIM = 64
RMS_EPS = 1e-6
RG_C = 8.0
CONV_W = 4

V7X_LANES = 128
V7X_SUBLANES = 8
V7X_VMEM_LIMIT_BYTES = 56 * 1024 * 1024
HEADS_PER_TILE = V7X_LANES // HEAD_DIM

EXP_ZERO_BELOW = -104.0
NEG_BIG = -1e30
FIXED_SHIFT_MAX_BOUND = 30.0

TOKEN_TILE = 512
TIME_TILE = 512
SB_TILE = 256
FOX_TILE = 512


def _params(semantics):
    return pltpu.CompilerParams(dimension_semantics=semantics,
                                vmem_limit_bytes=V7X_VMEM_LIMIT_BYTES)


def _resident(shape):
    nd = len(shape)
    return pl.BlockSpec(shape, lambda *_: (0,) * nd, pipeline_mode=pl.Buffered(1))


def _rmsnorm_rows(x, g):
    ms = jnp.mean(x * x, axis=-1, keepdims=True)
    return x * lax.rsqrt(ms + RMS_EPS) * g


def _softplus(x):
    return jnp.maximum(x, 0.0) + jnp.log(1.0 + jnp.exp(-jnp.abs(x)))


def _sigmoid(x):
    return 1.0 / (1.0 + jnp.exp(-x))


def _ffn_kernel(*refs, d_ff, n_parts):
    x_ref, g_ref, win_ref, wout_ref = refs[:4]
    o_ref = refs[-1]
    x = x_ref[...]
    for i in range(n_parts):
        x = x + jnp.dot(refs[4 + 2 * i][...], refs[5 + 2 * i][...], preferred_element_type=F32)
    h = _rmsnorm_rows(x, g_ref[...]).astype(BF16)
    ab = jnp.dot(h, win_ref[...], preferred_element_type=F32)
    a = ab[:, :d_ff]
    b = ab[:, d_ff:]
    t = (a * _sigmoid(a) * b).astype(BF16)
    y = jnp.dot(t, wout_ref[...], preferred_element_type=F32)
    o_ref[...] = x + 0.5 * y


def _ffn(x2, g, w_in, w_out, parts=()):
    t, d = x2.shape
    d_ff = w_out.shape[0]
    tm = min(TOKEN_TILE, t)
    row = lambda i: (i, 0)
    in_specs = [pl.BlockSpec((tm, d), row), _resident((1, d)), _resident(w_in.shape),
                _resident(w_out.shape)]
    args = [x2, g.reshape(1, d), w_in, w_out]
    for y, w in parts:
        in_specs += [pl.BlockSpec((tm, y.shape[1]), row), _resident(w.shape)]
        args += [y, w]
    return pl.pallas_call(
        functools.partial(_ffn_kernel, d_ff=d_ff, n_parts=len(parts)),
        out_shape=jax.ShapeDtypeStruct((t, d), F32),
        grid=(t // tm,),
        in_specs=in_specs,
        out_specs=pl.BlockSpec((tm, d), row),
        compiler_params=_params(("parallel",)),
        name="ffn",
    )(*args)


def _head_rmsnorm(t, gain_row, first_head):
    outs = []
    for j in range(t.shape[1] // V7X_LANES):
        tj = t[:, j * V7X_LANES:(j + 1) * V7X_LANES]
        sq = tj * tj
        s0 = jnp.sum(jnp.where(first_head, sq, 0.0), axis=-1, keepdims=True)
        s1 = jnp.sum(jnp.where(first_head, 0.0, sq), axis=-1, keepdims=True)
        ms = jnp.where(first_head, s0, s1) * (1.0 / HEAD_DIM)
        outs.append(tj * lax.rsqrt(ms + RMS_EPS) * gain_row)
    return jnp.concatenate(outs, axis=-1)


def _hyin_kernel(x_ref, g_ref, w_ref, qkg_ref, fb_ref,
                 ug_ref, q_ref, k_ref, v_ref, lf_ref, *, d_rnn, fox_w):
    x = x_ref[...]
    h = _rmsnorm_rows(x, g_ref[...]).astype(BF16)
    p = jnp.dot(h, w_ref[...], preferred_element_type=F32)
    o = 2 * d_rnn
    ug_ref[...] = p[:, :o]
    first_head = lax.broadcasted_iota(jnp.int32, (x.shape[0], V7X_LANES), 1) < HEAD_DIM
    scale = HEAD_DIM ** -0.5
    q = _head_rmsnorm(p[:, o:o + fox_w], qkg_ref[0:1, :], first_head)
    q_ref[...] = (q * scale).astype(BF16)
    k = _head_rmsnorm(p[:, o + fox_w:o + 2 * fox_w], qkg_ref[1:2, :], first_head)
    k_ref[...] = k.astype(BF16)
    v_ref[...] = p[:, o + 2 * fox_w:o + 3 * fox_w].astype(BF16)
    f = p[:, o + 3 * fox_w:] + fb_ref[...]
    lf_ref[...] = -_softplus(-f)


def _hyin(x2, g, w_pad, qk_gain, fb_pad, d_rnn, fox_w):
    t, d = x2.shape
    tm = min(TOKEN_TILE, t)
    row = lambda i: (i, 0)
    return pl.pallas_call(
        functools.partial(_hyin_kernel, d_rnn=d_rnn, fox_w=fox_w),
        out_shape=(jax.ShapeDtypeStruct((t, 2 * d_rnn), F32),
                   jax.ShapeDtypeStruct((t, fox_w), BF16),
                   jax.ShapeDtypeStruct((t, fox_w), BF16),
                   jax.ShapeDtypeStruct((t, fox_w), BF16),
                   jax.ShapeDtypeStruct((t, V7X_LANES), F32)),
        grid=(t // tm,),
        in_specs=[pl.BlockSpec((tm, d), row),
                  _resident((1, d)),
                  _resident(w_pad.shape),
                  _resident(qk_gain.shape),
                  _resident(fb_pad.shape)],
        out_specs=(pl.BlockSpec((tm, 2 * d_rnn), row),
                   pl.BlockSpec((tm, fox_w), row),
                   pl.BlockSpec((tm, fox_w), row),
                   pl.BlockSpec((tm, fox_w), row),
                   pl.BlockSpec((tm, V7X_LANES), row)),
        compiler_params=_params(("parallel",)),
        name="hyin",
    )(x2, g.reshape(1, d), w_pad, qk_gain, fb_pad)


def _scan_linear(a, b, row):
    n = a.shape[0]
    s = 1
    while s < n:
        keep = row >= s
        b = jnp.where(keep, a * pltpu.roll(b, s, 0) + b, b)
        a = jnp.where(keep, a * pltpu.roll(a, s, 0), a)
        s *= 2
    return a, b


def _cumsum_rows(x, row):
    n = x.shape[0]
    s = 1
    while s < n:
        x = jnp.where(row >= s, x + pltpu.roll(x, s, 0), x)
        s *= 2
    return x


def _rglru_kernel(ug_ref, lf_ref, cw_ref, cb_ref, gw_ref, gb_ref, lam_ref,
                  y_ref, c_ref, ct_ref, tail_ref, hc_ref, cc_ref, *, d_rnn, n_heads):
    @pl.when(pl.program_id(1) == 0)
    def _():
        tail_ref[...] = jnp.zeros_like(tail_ref)
        hc_ref[...] = jnp.zeros_like(hc_ref)
        cc_ref[...] = jnp.zeros_like(cc_ref)

    u = ug_ref[0, :, :d_rnn]
    gate = ug_ref[0, :, d_rnn:]
    tt = u.shape[0]
    row = lax.broadcasted_iota(jnp.int32, (tt, d_rnn), 0)

    tail = tail_ref[...]
    head_row = lax.broadcasted_iota(jnp.int32, (V7X_SUBLANES, d_rnn), 0)
    xc = cb_ref[...] + cw_ref[CONV_W - 1:CONV_W, :] * u
    for k in range(1, CONV_W):
        r = pltpu.roll(u, k, 0)
        head = jnp.where(head_row < k, pltpu.roll(tail, k, 0), r[:V7X_SUBLANES])
        shifted = jnp.concatenate([head, r[V7X_SUBLANES:]], axis=0)
        xc = xc + cw_ref[CONV_W - 1 - k:CONV_W - k, :] * shifted
    tail_ref[...] = u[tt - V7X_SUBLANES:, :]

    g = jnp.dot(xc.astype(BF16), gw_ref[...], preferred_element_type=F32) + gb_ref[...]
    r_gate = _sigmoid(g[:, :d_rnn])
    i_gate = _sigmoid(g[:, d_rnn:])
    log_a = -RG_C * r_gate * _softplus(-lam_ref[...])
    a = jnp.exp(log_a)
    one_minus_a2 = -jnp.tanh(log_a) * (a * a + 1.0)
    b = jnp.sqrt(one_minus_a2) * (i_gate * xc)

    big_a, big_b = _scan_linear(a, b, row)
    hseq = big_a * hc_ref[...] + big_b
    hc_ref[...] = hseq[tt - 1:tt, :]
    y_ref[0] = (hseq * jax.nn.gelu(gate)).astype(BF16)

    lrow = lax.broadcasted_iota(jnp.int32, (tt, V7X_LANES), 0)
    c = _cumsum_rows(lf_ref[0], lrow) + cc_ref[...]
    cc_ref[...] = c[tt - 1:tt, :]
    c_ref[0] = c
    ct_ref[0] = c.T[:n_heads, :]


def _rglru(ug, lf, conv_w, conv_b, gate_w_dense, gate_b, lam, n_heads):
    bsz, s, two_d = ug.shape
    d_rnn = two_d // 2
    tt = min(TIME_TILE, s)
    blk = lambda b, i: (b, i, 0)
    return pl.pallas_call(
        functools.partial(_rglru_kernel, d_rnn=d_rnn, n_heads=n_heads),
        out_shape=(jax.ShapeDtypeStruct((bsz, s, d_rnn), BF16),
                   jax.ShapeDtypeStruct((bsz, s, V7X_LANES), F32),
                   jax.ShapeDtypeStruct((bsz, n_heads, s), F32)),
        grid=(bsz, s // tt),
        in_specs=[pl.BlockSpec((1, tt, two_d), blk),
                  pl.BlockSpec((1, tt, V7X_LANES), blk),
                  _resident(conv_w.shape),
                  _resident((1, d_rnn)),
                  _resident(gate_w_dense.shape),
                  _resident((1, 2 * d_rnn)),
                  _resident((1, d_rnn))],
        out_specs=(pl.BlockSpec((1, tt, d_rnn), blk),
                   pl.BlockSpec((1, tt, V7X_LANES), blk),
                   pl.BlockSpec((1, n_heads, tt), lambda b, i: (b, 0, i))),
        scratch_shapes=[pltpu.VMEM((V7X_SUBLANES, d_rnn), F32),
                        pltpu.VMEM((1, d_rnn), F32),
                        pltpu.VMEM((1, V7X_LANES), F32)],
        compiler_params=_params(("parallel", "arbitrary")),
        name="rglru",
    )(ug, lf, conv_w, conv_b.reshape(1, d_rnn), gate_w_dense,
      gate_b.reshape(1, 2 * d_rnn), lam.reshape(1, d_rnn))


def _qk(qm, kj):
    return lax.dot_general(qm, kj, (((1,), (1,)), ((), ())), preferred_element_type=F32)


def _max_key_norms(k_ref, kmax_ref, chunk):
    s = k_ref.shape[1]
    first = lax.broadcasted_iota(jnp.int32, (chunk, V7X_LANES), 1) < HEAD_DIM
    first_row = lax.broadcasted_iota(jnp.int32, (1, V7X_LANES), 1) < HEAD_DIM
    m0 = jnp.zeros((1, 1), F32)
    m1 = jnp.zeros((1, 1), F32)
    for c in range(s // chunk):
        kk = k_ref[0, c * chunk:(c + 1) * chunk, :].astype(F32)
        sq = kk * kk
        n0 = jnp.sum(jnp.where(first, sq, 0.0), axis=-1, keepdims=True)
        n1 = jnp.sum(jnp.where(first, 0.0, sq), axis=-1, keepdims=True)
        m0 = jnp.maximum(m0, jnp.max(n0, axis=0, keepdims=True))
        m1 = jnp.maximum(m1, jnp.max(n1, axis=0, keepdims=True))
    kmax_ref[...] = jnp.sqrt(jnp.where(first_row, m0, m1))


def _logit_bound(qm, kmax_row, head_mask_row):
    qf = qm.astype(F32)
    qn = jnp.sqrt(jnp.sum(qf * qf, axis=-1, keepdims=True))
    km = jnp.max(jnp.where(head_mask_row, kmax_row, 0.0), axis=-1, keepdims=True)
    return qn * km * 1.01 + 1e-3


def _head_masks(rows):
    lane = lax.broadcasted_iota(jnp.int32, (rows, V7X_LANES), 1)
    lane_row = lax.broadcasted_iota(jnp.int32, (1, V7X_LANES), 1)
    full = [(lane >= hd * HEAD_DIM) & (lane < (hd + 1) * HEAD_DIM) for hd in range(HEADS_PER_TILE)]
    one = [(lane_row >= hd * HEAD_DIM) & (lane_row < (hd + 1) * HEAD_DIM)
           for hd in range(HEADS_PER_TILE)]
    return lane, full, one


def _sb_kernel(q_ref, k_ref, v_ref, o_ref, kmax_ref):
    qi = pl.program_id(2)
    tq = q_ref.shape[1]
    tk = tq

    @pl.when(qi == 0)
    def _():
        _max_key_norms(k_ref, kmax_ref, min(1024, k_ref.shape[1]))

    _, head_masks, head_rows = _head_masks(tq)
    rows = lax.broadcasted_iota(jnp.int32, (tq, tk), 0)
    cols = lax.broadcasted_iota(jnp.int32, (tq, tk), 1)
    strictly_past = cols < rows
    tri = jnp.where(rows >= cols, 1.0, 0.0).astype(BF16)
    q = q_ref[0]
    kmax_row = kmax_ref[...]

    def block(qm, j, run, acc, diag):
        start = pl.multiple_of(j * tk, tk)
        kj = k_ref[0, pl.ds(start, tk), :]
        vj = v_ref[0, pl.ds(start, tk), :]
        z = _qk(qm, kj)
        log_nb = -_softplus(z)
        if diag:
            log_nb = jnp.where(strictly_past, log_nb, 0.0)
        hi = log_nb.astype(BF16)
        lo = (log_nb - hi.astype(F32)).astype(BF16)
        incl = (jnp.dot(hi, tri, preferred_element_type=F32)
                + jnp.dot(lo, tri, preferred_element_type=F32))
        w = jnp.exp(z + incl + run)
        if diag:
            w = jnp.where(strictly_past, w, 0.0)
        acc = acc + jnp.dot(w.astype(BF16), vj, preferred_element_type=F32)
        run = run + incl[:, 0:1]
        return run, acc

    left = jnp.maximum(qi - 1, 0)
    left_offset = jnp.where(qi >= 1, 0.0, NEG_BIG)
    state = []
    for hd in range(HEADS_PER_TILE):
        qm = jnp.where(head_masks[hd], q, jnp.zeros_like(q))
        bound = _logit_bound(qm, kmax_row, head_rows[hd])
        run, acc = block(qm, qi, jnp.zeros((tq, 1), F32), jnp.zeros((tq, V7X_LANES), F32), True)
        run, acc = block(qm, left, run + left_offset, acc, False)
        state.append((qm, bound, run, acc))

    res = None
    for hd, (qm, bound, run, acc) in enumerate(state):

        def live(run, bound=bound):
            return jnp.max(run + bound) > EXP_ZERO_BELOW

        def cond(carry):
            return jnp.logical_and(carry[0] >= 0, carry[1])

        def body(carry, qm=qm, live=live):
            j, _, run, acc = carry
            run, acc = block(qm, j, run, acc, False)
            return j - 1, live(run), run, acc

        _, _, _, acc = lax.while_loop(cond, body, (qi - 2, live(run), run, acc))
        res = acc if res is None else jnp.where(head_masks[hd], acc, res)
    o_ref[0] = res.astype(BF16)


def _sb_attention(q, k, v):
    bsz, s, w = q.shape
    tq = min(SB_TILE, s)
    n_tiles = w // V7X_LANES
    return pl.pallas_call(
        _sb_kernel,
        out_shape=jax.ShapeDtypeStruct((bsz, s, w), BF16),
        grid=(bsz, n_tiles, s // tq),
        in_specs=[pl.BlockSpec((1, tq, V7X_LANES), lambda b, p, i: (b, i, p)),
                  pl.BlockSpec((1, s, V7X_LANES), lambda b, p, i: (b, 0, p)),
                  pl.BlockSpec((1, s, V7X_LANES), lambda b, p, i: (b, 0, p))],
        out_specs=pl.BlockSpec((1, tq, V7X_LANES), lambda b, p, i: (b, i, p)),
        scratch_shapes=[pltpu.VMEM((1, V7X_LANES), F32)],
        compiler_params=_params(("parallel", "parallel", "arbitrary")),
        name="sb_attn",
    )(q, k, v)


def _fox_kernel(q_ref, k_ref, v_ref, c_ref, ct_ref, o_ref, kmax_ref):
    tile = pl.program_id(1)
    qi = pl.program_id(2)
    tq = q_ref.shape[1]
    tk = tq

    @pl.when(qi == 0)
    def _():
        _max_key_norms(k_ref, kmax_ref, min(1024, k_ref.shape[1]))

    lane, head_masks, head_rows = _head_masks(tq)
    _, key_masks, _ = _head_masks(tk)
    rows = lax.broadcasted_iota(jnp.int32, (tq, tk), 0)
    cols = lax.broadcasted_iota(jnp.int32, (tq, tk), 1)
    causal = cols <= rows
    q = q_ref[0]
    c_blk = c_ref[0]
    kmax_row = kmax_ref[...]

    def load(head, j):
        start = pl.multiple_of(j * tk, tk)
        kj = k_ref[0, pl.ds(start, tk), :]
        vj = v_ref[0, pl.ds(start, tk), :]
        ck = ct_ref[0, pl.ds(head, 1), pl.ds(start, tk)]
        return kj, vj, ck

    def fixed_shift(qm, hd, head, cq, bound):
        shift = cq - bound
        cq_top = jnp.max(cq)

        def block(j, acc, diag):
            kj, vj, ck = load(head, j)
            s = _qk(qm, kj) + shift - ck
            if diag:
                s = jnp.where(causal, s, NEG_BIG)
            p = jnp.exp(s)
            v1 = jnp.where(key_masks[hd], vj, jnp.ones_like(vj))
            acc = acc + jnp.dot(p.astype(BF16), v1, preferred_element_type=F32)
            return acc, cq_top - jnp.max(ck) > EXP_ZERO_BELOW

        acc, go = block(qi, jnp.zeros((tq, V7X_LANES), F32), True)

        def cond(carry):
            return jnp.logical_and(carry[0] >= 0, carry[1])

        def body(carry):
            j, _, acc = carry
            acc, go = block(j, acc, False)
            return j - 1, go, acc

        _, _, acc = lax.while_loop(cond, body, (qi - 1, go, acc))
        return acc / pltpu.roll(acc, HEAD_DIM, 1)

    def running_shift(qm, hd, head, cq, bound):
        del hd

        def block(j, m, l, acc, diag):
            kj, vj, ck = load(head, j)
            s = _qk(qm, kj) + cq - ck
            if diag:
                s = jnp.where(causal, s, NEG_BIG)
            m_new = jnp.maximum(m, jnp.max(s, axis=-1, keepdims=True))
            alpha = jnp.exp(m - m_new)
            p = jnp.exp(s - m_new)
            l = alpha * l + jnp.sum(p, axis=-1, keepdims=True)
            acc = alpha * acc + jnp.dot(p.astype(BF16), vj, preferred_element_type=F32)
            reach = cq - jnp.max(ck, axis=-1, keepdims=True)
            return m_new, l, acc, jnp.max(bound + reach - m_new) > EXP_ZERO_BELOW

        m, l, acc, go = block(qi, jnp.full((tq, 1), NEG_BIG, F32), jnp.zeros((tq, 1), F32),
                              jnp.zeros((tq, V7X_LANES), F32), True)

        def cond(carry):
            return jnp.logical_and(carry[0] >= 0, carry[1])

        def body(carry):
            j, _, m, l, acc = carry
            m, l, acc, go = block(j, m, l, acc, False)
            return j - 1, go, m, l, acc

        _, _, _, l, acc = lax.while_loop(cond, body, (qi - 1, go, m, l, acc))
        return acc / l

    res = None
    for hd in range(HEADS_PER_TILE):
        head = tile * HEADS_PER_TILE + hd
        qm = jnp.where(head_masks[hd], q, jnp.zeros_like(q))
        bound = _logit_bound(qm, kmax_row, head_rows[hd])
        cq = jnp.sum(jnp.where(lane == head, c_blk, 0.0), axis=-1, keepdims=True)
        out = lax.cond(jnp.max(bound) <= FIXED_SHIFT_MAX_BOUND,
                       functools.partial(fixed_shift, qm, hd, head),
                       functools.partial(running_shift, qm, hd, head),
                       cq, bound)
        res = out if res is None else jnp.where(head_masks[hd], out, res)
    o_ref[0] = res.astype(BF16)


def _fox_attention(q, k, v, c, ct):
    bsz, s, w = q.shape
    n_heads = ct.shape[1]
    tq = min(FOX_TILE, s)
    n_tiles = w // V7X_LANES
    return pl.pallas_call(
        _fox_kernel,
        out_shape=jax.ShapeDtypeStruct((bsz, s, w), BF16),
        grid=(bsz, n_tiles, s // tq),
        in_specs=[pl.BlockSpec((1, tq, V7X_LANES), lambda b, p, i: (b, i, p)),
                  pl.BlockSpec((1, s, V7X_LANES), lambda b, p, i: (b, 0, p)),
                  pl.BlockSpec((1, s, V7X_LANES), lambda b, p, i: (b, 0, p)),
                  pl.BlockSpec((1, tq, V7X_LANES), lambda b, p, i: (b, i, 0)),
                  pl.BlockSpec((1, n_heads, s), lambda b, p, i: (b, 0, 0))],
        out_specs=pl.BlockSpec((1, tq, V7X_LANES), lambda b, p, i: (b, i, p)),
        scratch_shapes=[pltpu.VMEM((1, V7X_LANES), F32)],
        compiler_params=_params(("parallel", "parallel", "arbitrary")),
        name="fox_attn",
    )(q, k, v, c, ct)


def _sbin_kernel(x_ref, g_ref, w_ref, q_ref, k_ref, v_ref, *, width):
    x = x_ref[...]
    h = _rmsnorm_rows(x, g_ref[...]).astype(BF16)
    p = jnp.dot(h, w_ref[...], preferred_element_type=F32)
    q_ref[...] = (p[:, :width] * (HEAD_DIM ** -0.5)).astype(BF16)
    k_ref[...] = p[:, width:2 * width].astype(BF16)
    v_ref[...] = p[:, 2 * width:].astype(BF16)


def _sbin(x2, g, w_qkv):
    t, d = x2.shape
    width = w_qkv.shape[1] // 3
    tm = min(TOKEN_TILE, t)
    row = lambda i: (i, 0)
    return pl.pallas_call(
        functools.partial(_sbin_kernel, width=width),
        out_shape=tuple(jax.ShapeDtypeStruct((t, width), BF16) for _ in range(3)),
        grid=(t // tm,),
        in_specs=[pl.BlockSpec((tm, d), row), _resident((1, d)), _resident(w_qkv.shape)],
        out_specs=tuple(pl.BlockSpec((tm, width), row) for _ in range(3)),
        compiler_params=_params(("parallel",)),
        name="sbin",
    )(x2, g.reshape(1, d), w_qkv)


def _block_diag_gates(gate_w):
    n_gates, n_blocks, r, _ = gate_w.shape
    eye = jnp.eye(n_blocks, dtype=gate_w.dtype)
    dense = jnp.einsum('ghij,hk->ghikj', gate_w, eye).reshape(n_gates, n_blocks * r, n_blocks * r)
    return jnp.concatenate([dense[g] for g in range(n_gates)], axis=1)


def kernel(x, ffn_norm, ffn_w_in, ffn_w_out, mix_norm, hy_w_in, rg_conv_w, rg_conv_b, rg_gate_w,
           rg_gate_b, rg_lambda, fox_fgate_b, fox_qk_norm, hy_w_out, sb_w_qkv, sb_w_out):
    bsz, s, d = x.shape
    depth = ffn_norm.shape[0]
    d_rnn = rg_conv_w.shape[2]
    n_fox = fox_fgate_b.shape[1]
    fox_w = n_fox * HEAD_DIM
    t = bsz * s

    x2 = x.reshape(t, d)
    for layer in range(depth):
        x2 = _ffn(x2, ffn_norm[layer, 0], ffn_w_in[layer, 0].astype(BF16),
                  ffn_w_out[layer, 0].astype(BF16))
        if layer % 2 == 0:
            e = layer // 2
            pad = V7X_LANES - n_fox
            w_pad = jnp.pad(hy_w_in[e], ((0, 0), (0, pad))).astype(BF16)
            fb_pad = jnp.pad(fox_fgate_b[e], (0, pad)).reshape(1, V7X_LANES)
            qk_gain = jnp.tile(fox_qk_norm[e], (1, HEADS_PER_TILE))
            ug, q, k, v, lf = _hyin(x2, mix_norm[layer], w_pad, qk_gain, fb_pad, d_rnn, fox_w)
            y_rnn, c, ct = _rglru(ug.reshape(bsz, s, 2 * d_rnn), lf.reshape(bsz, s, V7X_LANES),
                                  rg_conv_w[e], rg_conv_b[e],
                                  _block_diag_gates(rg_gate_w[e]).astype(BF16),
                                  rg_gate_b[e].reshape(-1), rg_lambda[e], n_fox)
            y_fox = _fox_attention(q.reshape(bsz, s, fox_w), k.reshape(bsz, s, fox_w),
                                   v.reshape(bsz, s, fox_w), c, ct)
            w_out = hy_w_out[e].astype(BF16)
            mixer_out = [(y_rnn.reshape(t, d_rnn), w_out[:d_rnn]),
                         (y_fox.reshape(t, fox_w), w_out[d_rnn:])]
        else:
            o = layer // 2
            q, k, v = _sbin(x2, mix_norm[layer], sb_w_qkv[o].astype(BF16))
            width = q.shape[1]
            y = _sb_attention(q.reshape(bsz, s, width), k.reshape(bsz, s, width),
                              v.reshape(bsz, s, width))
            mixer_out = [(y.reshape(t, width), sb_w_out[o].astype(BF16))]
        x2 = _ffn(x2, ffn_norm[layer, 1], ffn_w_in[layer, 1].astype(BF16),
                  ffn_w_out[layer, 1].astype(BF16), mixer_out)
    return x2.reshape(bsz, s, d)
```

```python
import functools

import jax
import jax.numpy as jnp
from jax import lax
from jax.experimental import pallas as pl
from jax.experimental.pallas import tpu as pltpu

F32 = jnp.float32
BF16 = jnp.bfloat16

HEAD_DIM = 64
RMS_EPS = 1e-6
RG_C = 8.0
CONV_W = 4

V7X_LANES = 128
V7X_SUBLANES = 8
V7X_VMEM_LIMIT_BYTES = 56 * 1024 * 1024
HEADS_PER_TILE = V7X_LANES // HEAD_DIM

EXP_ZERO_BELOW = -104.0
NEG_BIG = -1e30
LOG2E = 1.4426950408889634
FIXED_SHIFT_MAX_BOUND = 30.0

TOKEN_TILE = 512
TIME_TILE = 512
SB_TILE = 256
SB_SUBTILES = 2
FOX_TILE = 512


def _params(semantics):
    return pltpu.CompilerParams(dimension_semantics=semantics,
                                vmem_limit_bytes=V7X_VMEM_LIMIT_BYTES)


def _resident(shape):
    nd = len(shape)
    return pl.BlockSpec(shape, lambda *_: (0,) * nd, pipeline_mode=pl.Buffered(1))


def _rmsnorm_rows(x, g):
    ms = jnp.mean(x * x, axis=-1, keepdims=True)
    return x * lax.rsqrt(ms + RMS_EPS) * g


def _softplus(x):
    return jnp.maximum(x, 0.0) + jnp.log(1.0 + jnp.exp(-jnp.abs(x)))


def _sigmoid(x):
    return 1.0 / (1.0 + jnp.exp(-x))


def _log_one_minus_sigmoid(z):
    return -jnp.maximum(z, 0.0) - jnp.log(1.0 + jnp.exp2(jnp.abs(z) * (-LOG2E)))


def _ffn_kernel(*refs, d_ff, n_parts):
    x_ref, g_ref, win_ref, wout_ref = refs[:4]
    o_ref = refs[-1]
    x = x_ref[...]
    for i in range(n_parts):
        x = x + jnp.dot(refs[4 + 2 * i][...], refs[5 + 2 * i][...], preferred_element_type=F32)
    h = _rmsnorm_rows(x, g_ref[...]).astype(BF16)
    ab = jnp.dot(h, win_ref[...], preferred_element_type=F32)
    a = ab[:, :d_ff]
    b = ab[:, d_ff:]
    t = (a * _sigmoid(a) * b).astype(BF16)
    y = jnp.dot(t, wout_ref[...], preferred_element_type=F32)
    o_ref[...] = x + 0.5 * y


def _ffn(x2, g, w_in, w_out, parts=()):
    t, d = x2.shape
    d_ff = w_out.shape[0]
    tm = min(TOKEN_TILE, t)
    row = lambda i: (i, 0)
    in_specs = [pl.BlockSpec((tm, d), row), _resident((1, d)), _resident(w_in.shape),
                _resident(w_out.shape)]
    args = [x2, g.reshape(1, d), w_in, w_out]
    for y, w in parts:
        in_specs += [pl.BlockSpec((tm, y.shape[1]), row), _resident(w.shape)]
        args += [y, w]
    return pl.pallas_call(
        functools.partial(_ffn_kernel, d_ff=d_ff, n_parts=len(parts)),
        out_shape=jax.ShapeDtypeStruct((t, d), F32),
        grid=(t // tm,),
        in_specs=in_specs,
        out_specs=pl.BlockSpec((tm, d), row),
        compiler_params=_params(("parallel",)),
        name="ffn",
    )(*args)


def _head_rmsnorm(t, gain_row, first_head):
    outs = []
    for j in range(t.shape[1] // V7X_LANES):
        tj = t[:, j * V7X_LANES:(j + 1) * V7X_LANES]
        sq = tj * tj
        s0 = jnp.sum(jnp.where(first_head, sq, 0.0), axis=-1, keepdims=True)
        s1 = jnp.sum(jnp.where(first_head, 0.0, sq), axis=-1, keepdims=True)
        ms = jnp.where(first_head, s0, s1) * (1.0 / HEAD_DIM)
        outs.append(tj * lax.rsqrt(ms + RMS_EPS) * gain_row)
    return jnp.concatenate(outs, axis=-1)


def _hyin_kernel(x_ref, g_ref, w_ref, qkg_ref, fb_ref,
                 ug_ref, q_ref, k_ref, v_ref, lf_ref, *, d_rnn, fox_w):
    x = x_ref[...]
    h = _rmsnorm_rows(x, g_ref[...]).astype(BF16)
    p = jnp.dot(h, w_ref[...], preferred_element_type=F32)
    o = 2 * d_rnn
    ug_ref[...] = p[:, :o]
    first_head = lax.broadcasted_iota(jnp.int32, (x.shape[0], V7X_LANES), 1) < HEAD_DIM
    scale = HEAD_DIM ** -0.5
    q = _head_rmsnorm(p[:, o:o + fox_w], qkg_ref[0:1, :], first_head)
    q_ref[...] = (q * scale).astype(BF16)
    k = _head_rmsnorm(p[:, o + fox_w:o + 2 * fox_w], qkg_ref[1:2, :], first_head)
    k_ref[...] = k.astype(BF16)
    v_ref[...] = p[:, o + 2 * fox_w:o + 3 * fox_w].astype(BF16)
    f = p[:, o + 3 * fox_w:] + fb_ref[...]
    lf_ref[...] = -_softplus(-f)


def _hyin(x2, g, w_pad, qk_gain, fb_pad, d_rnn, fox_w):
    t, d = x2.shape
    tm = min(TOKEN_TILE, t)
    row = lambda i: (i, 0)
    return pl.pallas_call(
        functools.partial(_hyin_kernel, d_rnn=d_rnn, fox_w=fox_w),
        out_shape=(jax.ShapeDtypeStruct((t, 2 * d_rnn), F32),
                   jax.ShapeDtypeStruct((t, fox_w), BF16),
                   jax.ShapeDtypeStruct((t, fox_w), BF16),
                   jax.ShapeDtypeStruct((t, fox_w), BF16),
                   jax.ShapeDtypeStruct((t, V7X_LANES), F32)),
        grid=(t // tm,),
        in_specs=[pl.BlockSpec((tm, d), row),
                  _resident((1, d)),
                  _resident(w_pad.shape),
                  _resident(qk_gain.shape),
                  _resident(fb_pad.shape)],
        out_specs=(pl.BlockSpec((tm, 2 * d_rnn), row),
                   pl.BlockSpec((tm, fox_w), row),
                   pl.BlockSpec((tm, fox_w), row),
                   pl.BlockSpec((tm, fox_w), row),
                   pl.BlockSpec((tm, V7X_LANES), row)),
        compiler_params=_params(("parallel",)),
        name="hyin",
    )(x2, g.reshape(1, d), w_pad, qk_gain, fb_pad)


def _scan_linear(a, b, row):
    n = a.shape[0]
    s = 1
    while s < n:
        keep = row >= s
        b = jnp.where(keep, a * pltpu.roll(b, s, 0) + b, b)
        a = jnp.where(keep, a * pltpu.roll(a, s, 0), a)
        s *= 2
    return a, b


def _cumsum_rows(x, row):
    n = x.shape[0]
    s = 1
    while s < n:
        x = jnp.where(row >= s, x + pltpu.roll(x, s, 0), x)
        s *= 2
    return x


def _rglru_kernel(ug_ref, lf_ref, cw_ref, cb_ref, gw_ref, gb_ref, lam_ref,
                  y_ref, c_ref, ct_ref, tail_ref, hc_ref, cc_ref, *, d_rnn, n_heads):
    @pl.when(pl.program_id(1) == 0)
    def _():
        tail_ref[...] = jnp.zeros_like(tail_ref)
        hc_ref[...] = jnp.zeros_like(hc_ref)
        cc_ref[...] = jnp.zeros_like(cc_ref)

    u = ug_ref[0, :, :d_rnn]
    gate = ug_ref[0, :, d_rnn:]
    tt = u.shape[0]
    row = lax.broadcasted_iota(jnp.int32, (tt, d_rnn), 0)

    tail = tail_ref[...]
    head_row = lax.broadcasted_iota(jnp.int32, (V7X_SUBLANES, d_rnn), 0)
    xc = cb_ref[...] + cw_ref[CONV_W - 1:CONV_W, :] * u
    for k in range(1, CONV_W):
        r = pltpu.roll(u, k, 0)
        head = jnp.where(head_row < k, pltpu.roll(tail, k, 0), r[:V7X_SUBLANES])
        shifted = jnp.concatenate([head, r[V7X_SUBLANES:]], axis=0)
        xc = xc + cw_ref[CONV_W - 1 - k:CONV_W - k, :] * shifted
    tail_ref[...] = u[tt - V7X_SUBLANES:, :]

    g = jnp.dot(xc.astype(BF16), gw_ref[...], preferred_element_type=F32) + gb_ref[...]
    r_gate = _sigmoid(g[:, :d_rnn])
    i_gate = _sigmoid(g[:, d_rnn:])
    log_a = -RG_C * r_gate * _softplus(-lam_ref[...])
    a = jnp.exp(log_a)
    one_minus_a2 = -jnp.tanh(log_a) * (a * a + 1.0)
    b = jnp.sqrt(one_minus_a2) * (i_gate * xc)

    big_a, big_b = _scan_linear(a, b, row)
    hseq = big_a * hc_ref[...] + big_b
    hc_ref[...] = hseq[tt - 1:tt, :]
    y_ref[0] = (hseq * jax.nn.gelu(gate)).astype(BF16)

    lrow = lax.broadcasted_iota(jnp.int32, (tt, V7X_LANES), 0)
    c = _cumsum_rows(lf_ref[0], lrow) + cc_ref[...]
    cc_ref[...] = c[tt - 1:tt, :]
    c_ref[0] = c
    ct_ref[0] = c.T[:n_heads, :]


def _rglru(ug, lf, conv_w, conv_b, gate_w_dense, gate_b, lam, n_heads):
    bsz, s, two_d = ug.shape
    d_rnn = two_d // 2
    tt = min(TIME_TILE, s)
    blk = lambda b, i: (b, i, 0)
    return pl.pallas_call(
        functools.partial(_rglru_kernel, d_rnn=d_rnn, n_heads=n_heads),
        out_shape=(jax.ShapeDtypeStruct((bsz, s, d_rnn), BF16),
                   jax.ShapeDtypeStruct((bsz, s, V7X_LANES), F32),
                   jax.ShapeDtypeStruct((bsz, n_heads, s), F32)),
        grid=(bsz, s // tt),
        in_specs=[pl.BlockSpec((1, tt, two_d), blk),
                  pl.BlockSpec((1, tt, V7X_LANES), blk),
                  _resident(conv_w.shape),
                  _resident((1, d_rnn)),
                  _resident(gate_w_dense.shape),
                  _resident((1, 2 * d_rnn)),
                  _resident((1, d_rnn))],
        out_specs=(pl.BlockSpec((1, tt, d_rnn), blk),
                   pl.BlockSpec((1, tt, V7X_LANES), blk),
                   pl.BlockSpec((1, n_heads, tt), lambda b, i: (b, 0, i))),
        scratch_shapes=[pltpu.VMEM((V7X_SUBLANES, d_rnn), F32),
                        pltpu.VMEM((1, d_rnn), F32),
                        pltpu.VMEM((1, V7X_LANES), F32)],
        compiler_params=_params(("parallel", "arbitrary")),
        name="rglru",
    )(ug, lf, conv_w, conv_b.reshape(1, d_rnn), gate_w_dense,
      gate_b.reshape(1, 2 * d_rnn), lam.reshape(1, d_rnn))


def _qk(qm, kj):
    return lax.dot_general(qm, kj, (((1,), (1,)), ((), ())), preferred_element_type=F32)


def _max_key_norms(k_ref, kmax_ref, chunk):
    s = k_ref.shape[1]
    first = lax.broadcasted_iota(jnp.int32, (chunk, V7X_LANES), 1) < HEAD_DIM
    first_row = lax.broadcasted_iota(jnp.int32, (1, V7X_LANES), 1) < HEAD_DIM
    m0 = jnp.zeros((1, 1), F32)
    m1 = jnp.zeros((1, 1), F32)
    for c in range(s // chunk):
        kk = k_ref[0, c * chunk:(c + 1) * chunk, :].astype(F32)
        sq = kk * kk
        n0 = jnp.sum(jnp.where(first, sq, 0.0), axis=-1, keepdims=True)
        n1 = jnp.sum(jnp.where(first, 0.0, sq), axis=-1, keepdims=True)
        m0 = jnp.maximum(m0, jnp.max(n0, axis=0, keepdims=True))
        m1 = jnp.maximum(m1, jnp.max(n1, axis=0, keepdims=True))
    kmax_ref[...] = jnp.sqrt(jnp.where(first_row, m0, m1))


def _logit_bound(qm, kmax_row, head_mask_row):
    qf = qm.astype(F32)
    qn = jnp.sqrt(jnp.sum(qf * qf, axis=-1, keepdims=True))
    km = jnp.max(jnp.where(head_mask_row, kmax_row, 0.0), axis=-1, keepdims=True)
    return qn * km * 1.01 + 1e-3


def _head_masks(rows):
    lane = lax.broadcasted_iota(jnp.int32, (rows, V7X_LANES), 1)
    lane_row = lax.broadcasted_iota(jnp.int32, (1, V7X_LANES), 1)
    full = [(lane >= hd * HEAD_DIM) & (lane < (hd + 1) * HEAD_DIM) for hd in range(HEADS_PER_TILE)]
    one = [(lane_row >= hd * HEAD_DIM) & (lane_row < (hd + 1) * HEAD_DIM)
           for hd in range(HEADS_PER_TILE)]
    return lane, full, one


def _sb_kernel(q_ref, k_ref, v_ref, o_ref, kmax_ref, *, tile):
    qi = pl.program_id(2)
    n_sub = q_ref.shape[1] // tile
    tq = tk = tile

    @pl.when(qi == 0)
    def _():
        _max_key_norms(k_ref, kmax_ref, min(1024, k_ref.shape[1]))

    _, head_masks, head_rows = _head_masks(tq)
    rows = lax.broadcasted_iota(jnp.int32, (tq, tk), 0)
    cols = lax.broadcasted_iota(jnp.int32, (tq, tk), 1)
    strictly_past = cols < rows
    tri = jnp.where(rows >= cols, 1.0, 0.0).astype(BF16)
    kmax_row = kmax_ref[...]

    def block(qm, j, run, acc, diag):
        start = pl.multiple_of(j * tk, tk)
        kj = k_ref[0, pl.ds(start, tk), :]
        vj = v_ref[0, pl.ds(start, tk), :]
        z = _qk(qm, kj)
        if diag:
            z = jnp.where(strictly_past, z, NEG_BIG)
        log_nb = _log_one_minus_sigmoid(z)
        incl = jnp.dot(log_nb.astype(BF16), tri, preferred_element_type=F32)
        w = jnp.exp(z + incl + run)
        acc = acc + jnp.dot(w.astype(BF16), vj, preferred_element_type=F32)
        run = run + incl[:, 0:1]
        return run, acc

    state = []
    for sub in range(n_sub):
        blk = qi * n_sub + sub
        q = q_ref[0, sub * tile:(sub + 1) * tile, :]
        left = jnp.maximum(blk - 1, 0)
        left_offset = jnp.where(blk >= 1, 0.0, NEG_BIG)
        for hd in range(HEADS_PER_TILE):
            qm = jnp.where(head_masks[hd], q, jnp.zeros_like(q))
            bound = _logit_bound(qm, kmax_row, head_rows[hd])
            run, acc = block(qm, blk, jnp.zeros((tq, 1), F32), jnp.zeros((tq, V7X_LANES), F32), True)
            run, acc = block(qm, left, run + left_offset, acc, False)
            state.append((blk, qm, bound, run, acc, jnp.max(run + bound) > EXP_ZERO_BELOW))

    results = []
    for blk, qm, bound, run, acc, go in state:

        def cond(carry):
            return jnp.logical_and(carry[0] >= 0, carry[1])

        def body(carry, qm=qm, bound=bound):
            j, _, run, acc = carry
            run, acc = block(qm, j, run, acc, False)
            return j - 1, jnp.max(run + bound) > EXP_ZERO_BELOW, run, acc

        results.append(lax.while_loop(cond, body, (blk - 2, go, run, acc))[3])

    for sub in range(n_sub):
        res = results[sub * HEADS_PER_TILE]
        for hd in range(1, HEADS_PER_TILE):
            res = jnp.where(head_masks[hd], results[sub * HEADS_PER_TILE + hd], res)
        o_ref[0, sub * tile:(sub + 1) * tile, :] = res.astype(BF16)


def _sb_attention(q, k, v):
    bsz, s, w = q.shape
    tile = min(SB_TILE, s)
    tq = min(SB_TILE * SB_SUBTILES, s)
    n_tiles = w // V7X_LANES
    return pl.pallas_call(
        functools.partial(_sb_kernel, tile=tile),
        out_shape=jax.ShapeDtypeStruct((bsz, s, w), BF16),
        grid=(bsz, n_tiles, s // tq),
        in_specs=[pl.BlockSpec((1, tq, V7X_LANES), lambda b, p, i: (b, i, p)),
                  pl.BlockSpec((1, s, V7X_LANES), lambda b, p, i: (b, 0, p)),
                  pl.BlockSpec((1, s, V7X_LANES), lambda b, p, i: (b, 0, p))],
        out_specs=pl.BlockSpec((1, tq, V7X_LANES), lambda b, p, i: (b, i, p)),
        scratch_shapes=[pltpu.VMEM((1, V7X_LANES), F32)],
        compiler_params=_params(("parallel", "parallel", "arbitrary")),
        name="sb_attn",
    )(q, k, v)


def _fox_kernel(q_ref, k_ref, v_ref, c_ref, ct_ref, o_ref, kmax_ref):
    tile = pl.program_id(1)
    qi = pl.program_id(2)
    tq = q_ref.shape[1]
    tk = tq

    @pl.when(qi == 0)
    def _():
        _max_key_norms(k_ref, kmax_ref, min(1024, k_ref.shape[1]))

    lane, head_masks, head_rows = _head_masks(tq)
    _, key_masks, _ = _head_masks(tk)
    rows = lax.broadcasted_iota(jnp.int32, (tq, tk), 0)
    cols = lax.broadcasted_iota(jnp.int32, (tq, tk), 1)
    causal = cols <= rows
    q = q_ref[0]
    c_blk = c_ref[0]
    kmax_row = kmax_ref[...]

    def load(head, j):
        start = pl.multiple_of(j * tk, tk)
        kj = k_ref[0, pl.ds(start, tk), :]
        vj = v_ref[0, pl.ds(start, tk), :]
        ck = ct_ref[0, pl.ds(head, 1), pl.ds(start, tk)]
        return kj, vj, ck

    heads = []
    for hd in range(HEADS_PER_TILE):
        head = tile * HEADS_PER_TILE + hd
        qm = jnp.where(head_masks[hd], q, jnp.zeros_like(q))
        bound = _logit_bound(qm, kmax_row, head_rows[hd])
        cq = jnp.sum(jnp.where(lane == head, c_blk, 0.0), axis=-1, keepdims=True)
        heads.append((hd, head, qm, bound, cq))

    def merge(outs):
        res = outs[0]
        for hd in range(1, HEADS_PER_TILE):
            res = jnp.where(head_masks[hd], outs[hd], res)
        return res

    def fixed_shift():
        def block(hd, head, qm, shift, cq_top, j, acc, diag):
            kj, vj, ck = load(head, j)
            s = _qk(qm, kj) + shift - ck
            if diag:
                s = jnp.where(causal, s, NEG_BIG)
            p = jnp.exp(s)
            v1 = jnp.where(key_masks[hd], vj, jnp.ones_like(vj))
            acc = acc + jnp.dot(p.astype(BF16), v1, preferred_element_type=F32)
            return acc, cq_top - jnp.max(ck) > EXP_ZERO_BELOW

        left = jnp.maximum(qi - 1, 0)
        left_offset = jnp.where(qi >= 1, 0.0, NEG_BIG)
        state = []
        for hd, head, qm, bound, cq in heads:
            shift = cq - bound
            cq_top = jnp.max(cq)
            acc, _ = block(hd, head, qm, shift, cq_top, qi, jnp.zeros((tq, V7X_LANES), F32), True)
            acc, go = block(hd, head, qm, shift + left_offset, cq_top, left, acc, False)
            state.append((shift, cq_top, acc, go))

        outs = []
        for (hd, head, qm, _, _), (shift, cq_top, acc, go) in zip(heads, state):

            def cond(carry):
                return jnp.logical_and(carry[0] >= 0, carry[1])

            def body(carry, hd=hd, head=head, qm=qm, shift=shift, cq_top=cq_top):
                j, _, acc = carry
                acc, go = block(hd, head, qm, shift, cq_top, j, acc, False)
                return j - 1, go, acc

            _, _, acc = lax.while_loop(cond, body, (qi - 2, go, acc))
            outs.append(acc / pltpu.roll(acc, HEAD_DIM, 1))
        return merge(outs)

    def running_shift():
        outs = []
        for _, head, qm, bound, cq in heads:

            def block(j, m, l, acc, diag, head=head, qm=qm, bound=bound, cq=cq):
                kj, vj, ck = load(head, j)
                s = _qk(qm, kj) + cq - ck
                if diag:
                    s = jnp.where(causal, s, NEG_BIG)
                m_new = jnp.maximum(m, jnp.max(s, axis=-1, keepdims=True))
                alpha = jnp.exp(m - m_new)
                p = jnp.exp(s - m_new)
                l = alpha * l + jnp.sum(p, axis=-1, keepdims=True)
                acc = alpha * acc + jnp.dot(p.astype(BF16), vj, preferred_element_type=F32)
                reach = cq - jnp.max(ck, axis=-1, keepdims=True)
                return m_new, l, acc, jnp.max(bound + reach - m_new) > EXP_ZERO_BELOW

            m, l, acc, go = block(qi, jnp.full((tq, 1), NEG_BIG, F32), jnp.zeros((tq, 1), F32),
                                  jnp.zeros((tq, V7X_LANES), F32), True)

            def cond(carry):
                return jnp.logical_and(carry[0] >= 0, carry[1])

            def body(carry, block=block):
                j, _, m, l, acc = carry
                m, l, acc, go = block(j, m, l, acc, False)
                return j - 1, go, m, l, acc

            _, _, _, l, acc = lax.while_loop(cond, body, (qi - 1, go, m, l, acc))
            outs.append(acc / l)
        return merge(outs)

    largest_bound = functools.reduce(jnp.maximum, [jnp.max(h[3]) for h in heads])
    res = lax.cond(largest_bound <= FIXED_SHIFT_MAX_BOUND, fixed_shift, running_shift)
    o_ref[0] = res.astype(BF16)


def _fox_attention(q, k, v, c, ct):
    bsz, s, w = q.shape
    n_heads = ct.shape[1]
    tq = min(FOX_TILE, s)
    n_tiles = w // V7X_LANES
    return pl.pallas_call(
        _fox_kernel,
        out_shape=jax.ShapeDtypeStruct((bsz, s, w), BF16),
        grid=(bsz, n_tiles, s // tq),
        in_specs=[pl.BlockSpec((1, tq, V7X_LANES), lambda b, p, i: (b, i, p)),
                  pl.BlockSpec((1, s, V7X_LANES), lambda b, p, i: (b, 0, p)),
                  pl.BlockSpec((1, s, V7X_LANES), lambda b, p, i: (b, 0, p)),
                  pl.BlockSpec((1, tq, V7X_LANES), lambda b, p, i: (b, i, 0)),
                  pl.BlockSpec((1, n_heads, s), lambda b, p, i: (b, 0, 0))],
        out_specs=pl.BlockSpec((1, tq, V7X_LANES), lambda b, p, i: (b, i, p)),
        scratch_shapes=[pltpu.VMEM((1, V7X_LANES), F32)],
        compiler_params=_params(("parallel", "parallel", "arbitrary")),
        name="fox_attn",
    )(q, k, v, c, ct)


def _sbin_kernel(x_ref, g_ref, w_ref, q_ref, k_ref, v_ref, *, width):
    x = x_ref[...]
    h = _rmsnorm_rows(x, g_ref[...]).astype(BF16)
    p = jnp.dot(h, w_ref[...], preferred_element_type=F32)
    q_ref[...] = (p[:, :width] * (HEAD_DIM ** -0.5)).astype(BF16)
    k_ref[...] = p[:, width:2 * width].astype(BF16)
    v_ref[...] = p[:, 2 * width:].astype(BF16)


def _sbin(x2, g, w_qkv):
    t, d = x2.shape
    width = w_qkv.shape[1] // 3
    tm = min(TOKEN_TILE, t)
    row = lambda i: (i, 0)
    return pl.pallas_call(
        functools.partial(_sbin_kernel, width=width),
        out_shape=tuple(jax.ShapeDtypeStruct((t, width), BF16) for _ in range(3)),
        grid=(t // tm,),
        in_specs=[pl.BlockSpec((tm, d), row), _resident((1, d)), _resident(w_qkv.shape)],
        out_specs=tuple(pl.BlockSpec((tm, width), row) for _ in range(3)),
        compiler_params=_params(("parallel",)),
        name="sbin",
    )(x2, g.reshape(1, d), w_qkv)


def _block_diag_gates(gate_w):
    n_gates, n_blocks, r, _ = gate_w.shape
    eye = jnp.eye(n_blocks, dtype=gate_w.dtype)
    dense = jnp.einsum('ghij,hk->ghikj', gate_w, eye).reshape(n_gates, n_blocks * r, n_blocks * r)
    return jnp.concatenate([dense[g] for g in range(n_gates)], axis=1)


def kernel(x, ffn_norm, ffn_w_in, ffn_w_out, mix_norm, hy_w_in, rg_conv_w, rg_conv_b, rg_gate_w,
           rg_gate_b, rg_lambda, fox_fgate_b, fox_qk_norm, hy_w_out, sb_w_qkv, sb_w_out):
    bsz, s, d = x.shape
    depth = ffn_norm.shape[0]
    d_rnn = rg_conv_w.shape[2]
    n_fox = fox_fgate_b.shape[1]
    fox_w = n_fox * HEAD_DIM
    t = bsz * s

    x2 = x.reshape(t, d)
    for layer in range(depth):
        x2 = _ffn(x2, ffn_norm[layer, 0], ffn_w_in[layer, 0].astype(BF16),
                  ffn_w_out[layer, 0].astype(BF16))
        if layer % 2 == 0:
            e = layer // 2
            pad = V7X_LANES - n_fox
            w_pad = jnp.pad(hy_w_in[e], ((0, 0), (0, pad))).astype(BF16)
            fb_pad = jnp.pad(fox_fgate_b[e], (0, pad)).reshape(1, V7X_LANES)
            qk_gain = jnp.tile(fox_qk_norm[e], (1, HEADS_PER_TILE))
            ug, q, k, v, lf = _hyin(x2, mix_norm[layer], w_pad, qk_gain, fb_pad, d_rnn, fox_w)
            y_rnn, c, ct = _rglru(ug.reshape(bsz, s, 2 * d_rnn), lf.reshape(bsz, s, V7X_LANES),
                                  rg_conv_w[e], rg_conv_b[e],
                                  _block_diag_gates(rg_gate_w[e]).astype(BF16),
                                  rg_gate_b[e].reshape(-1), rg_lambda[e], n_fox)
            y_fox = _fox_attention(q.reshape(bsz, s, fox_w), k.reshape(bsz, s, fox_w),
                                   v.reshape(bsz, s, fox_w), c, ct)
            w_out = hy_w_out[e].astype(BF16)
            mixer_out = [(y_rnn.reshape(t, d_rnn), w_out[:d_rnn]),
                         (y_fox.reshape(t, fox_w), w_out[d_rnn:])]
        else:
            o = layer // 2
            q, k, v = _sbin(x2, mix_norm[layer], sb_w_qkv[o].astype(BF16))
            width = q.shape[1]
            y = _sb_attention(q.reshape(bsz, s, width), k.reshape(bsz, s, width),
                              v.reshape(bsz, s, width))
            mixer_out = [(y.reshape(t, width), sb_w_out[o].astype(BF16))]
        x2 = _ffn(x2, ffn_norm[layer, 1], ffn_w_in[layer, 1].astype(BF16),
                  ffn_w_out[layer, 1].astype(BF16), mixer_out)
    return x2.reshape(bsz, s, d)
```

```python
import functools

import jax
import jax.numpy as jnp
from jax import lax
from jax.experimental import pallas as pl
from jax.experimental.pallas import tpu as pltpu

F32 = jnp.float32
BF16 = jnp.bfloat16

HEAD_DIM = 64
RMS_EPS = 1e-6
RG_C = 8.0
CONV_W = 4

V7X_LANES = 128
V7X_SUBLANES = 8
V7X_VMEM_LIMIT_BYTES = 56 * 1024 * 1024
HEADS_PER_TILE = V7X_LANES // HEAD_DIM

EXP_ZERO_BELOW = -104.0
NEG_BIG = -1e30
LOG2E = 1.4426950408889634
FIXED_SHIFT_MAX_BOUND = 30.0

TOKEN_TILE = 512
TIME_TILE = 512
SCAN_CHUNK = 8
SB_TILE = 256
SB_SUBTILES = 2
FOX_TILE = 512
FOX_SUBTILES = 2


def _params(semantics):
    return pltpu.CompilerParams(dimension_semantics=semantics,
                                vmem_limit_bytes=V7X_VMEM_LIMIT_BYTES)


def _resident(shape):
    nd = len(shape)
    return pl.BlockSpec(shape, lambda *_: (0,) * nd, pipeline_mode=pl.Buffered(1))


def _rmsnorm_rows(x, g):
    ms = jnp.mean(x * x, axis=-1, keepdims=True)
    return x * lax.rsqrt(ms + RMS_EPS) * g


def _softplus(x):
    return jnp.maximum(x, 0.0) + jnp.log(1.0 + jnp.exp(-jnp.abs(x)))


def _sigmoid(x):
    return 1.0 / (1.0 + jnp.exp(-x))


def _log_sigmoid(x):
    return jnp.minimum(x, 0.0) - jnp.log(1.0 + jnp.exp2(jnp.abs(x) * (-LOG2E)))


def _ffn_kernel(*refs, d_ff, n_parts):
    x_ref, g_ref, win_ref, wout_ref = refs[:4]
    o_ref = refs[-1]
    x = x_ref[...]
    for i in range(n_parts):
        x = x + jnp.dot(refs[4 + 2 * i][...], refs[5 + 2 * i][...], preferred_element_type=F32)
    h = _rmsnorm_rows(x, g_ref[...]).astype(BF16)
    ab = jnp.dot(h, win_ref[...], preferred_element_type=F32)
    a = ab[:, :d_ff]
    b = ab[:, d_ff:]
    t = (a * _sigmoid(a) * b).astype(BF16)
    y = jnp.dot(t, wout_ref[...], preferred_element_type=F32)
    o_ref[...] = x + 0.5 * y


def _ffn(x2, g, w_in, w_out, parts=()):
    t, d = x2.shape
    d_ff = w_out.shape[0]
    tm = min(TOKEN_TILE, t)
    row = lambda i: (i, 0)
    in_specs = [pl.BlockSpec((tm, d), row), _resident((1, d)), _resident(w_in.shape),
                _resident(w_out.shape)]
    args = [x2, g.reshape(1, d), w_in, w_out]
    for y, w in parts:
        in_specs += [pl.BlockSpec((tm, y.shape[1]), row), _resident(w.shape)]
        args += [y, w]
    return pl.pallas_call(
        functools.partial(_ffn_kernel, d_ff=d_ff, n_parts=len(parts)),
        out_shape=jax.ShapeDtypeStruct((t, d), F32),
        grid=(t // tm,),
        in_specs=in_specs,
        out_specs=pl.BlockSpec((tm, d), row),
        compiler_params=_params(("parallel",)),
        name="ffn",
    )(*args)


def _head_rmsnorm(t, gain_row, first_head):
    outs = []
    for j in range(t.shape[1] // V7X_LANES):
        tj = t[:, j * V7X_LANES:(j + 1) * V7X_LANES]
        sq = tj * tj
        s0 = jnp.sum(jnp.where(first_head, sq, 0.0), axis=-1, keepdims=True)
        s1 = jnp.sum(jnp.where(first_head, 0.0, sq), axis=-1, keepdims=True)
        ms = jnp.where(first_head, s0, s1) * (1.0 / HEAD_DIM)
        outs.append(tj * lax.rsqrt(ms + RMS_EPS) * gain_row)
    return jnp.concatenate(outs, axis=-1)


def _hyin_kernel(x_ref, g_ref, w_ref, qkg_ref, fb_ref,
                 ug_ref, q_ref, k_ref, v_ref, lf_ref, *, d_rnn, fox_w):
    x = x_ref[...]
    h = _rmsnorm_rows(x, g_ref[...]).astype(BF16)
    p = jnp.dot(h, w_ref[...], preferred_element_type=F32)
    o = 2 * d_rnn
    ug_ref[...] = p[:, :o]
    first_head = lax.broadcasted_iota(jnp.int32, (x.shape[0], V7X_LANES), 1) < HEAD_DIM
    scale = HEAD_DIM ** -0.5
    q = _head_rmsnorm(p[:, o:o + fox_w], qkg_ref[0:1, :], first_head)
    q_ref[...] = (q * scale).astype(BF16)
    k = _head_rmsnorm(p[:, o + fox_w:o + 2 * fox_w], qkg_ref[1:2, :], first_head)
    k_ref[...] = k.astype(BF16)
    v_ref[...] = p[:, o + 2 * fox_w:o + 3 * fox_w].astype(BF16)
    f = p[:, o + 3 * fox_w:] + fb_ref[...]
    lf_ref[...] = _log_sigmoid(f)


def _hyin(x2, g, w_pad, qk_gain, fb_pad, d_rnn, fox_w):
    t, d = x2.shape
    tm = min(TOKEN_TILE, t)
    row = lambda i: (i, 0)
    return pl.pallas_call(
        functools.partial(_hyin_kernel, d_rnn=d_rnn, fox_w=fox_w),
        out_shape=(jax.ShapeDtypeStruct((t, 2 * d_rnn), F32),
                   jax.ShapeDtypeStruct((t, fox_w), BF16),
                   jax.ShapeDtypeStruct((t, fox_w), BF16),
                   jax.ShapeDtypeStruct((t, fox_w), BF16),
                   jax.ShapeDtypeStruct((t, V7X_LANES), F32)),
        grid=(t // tm,),
        in_specs=[pl.BlockSpec((tm, d), row),
                  _resident((1, d)),
                  _resident(w_pad.shape),
                  _resident(qk_gain.shape),
                  _resident(fb_pad.shape)],
        out_specs=(pl.BlockSpec((tm, 2 * d_rnn), row),
                   pl.BlockSpec((tm, fox_w), row),
                   pl.BlockSpec((tm, fox_w), row),
                   pl.BlockSpec((tm, fox_w), row),
                   pl.BlockSpec((tm, V7X_LANES), row)),
        compiler_params=_params(("parallel",)),
        name="hyin",
    )(x2, g.reshape(1, d), w_pad, qk_gain, fb_pad)


def _scan_linear(a, b, row):
    n = a.shape[0]
    s = 1
    while s < n:
        keep = row >= s
        b = jnp.where(keep, a * pltpu.roll(b, s, 0) + b, b)
        a = jnp.where(keep, a * pltpu.roll(a, s, 0), a)
        s *= 2
    return a, b


def _cumsum_rows(x, row):
    n = x.shape[0]
    s = 1
    while s < n:
        x = jnp.where(row >= s, x + pltpu.roll(x, s, 0), x)
        s *= 2
    return x


def _rglru_kernel(ug_ref, lf_ref, cw_ref, cb_ref, gw_ref, gb_ref, lam_ref,
                  y_ref, c_ref, ct_ref, tail_ref, hc_ref, cc_ref, *, d_rnn, n_heads):
    @pl.when(pl.program_id(1) == 0)
    def _():
        tail_ref[...] = jnp.zeros_like(tail_ref)
        hc_ref[...] = jnp.zeros_like(hc_ref)
        cc_ref[...] = jnp.zeros_like(cc_ref)

    u = ug_ref[0, :, :d_rnn]
    gate = ug_ref[0, :, d_rnn:]
    tt = u.shape[0]
    row = lax.broadcasted_iota(jnp.int32, (tt, d_rnn), 0)

    tail = tail_ref[...]
    head_row = lax.broadcasted_iota(jnp.int32, (V7X_SUBLANES, d_rnn), 0)
    xc = cb_ref[...] + cw_ref[CONV_W - 1:CONV_W, :] * u
    for k in range(1, CONV_W):
        r = pltpu.roll(u, k, 0)
        head = jnp.where(head_row < k, pltpu.roll(tail, k, 0), r[:V7X_SUBLANES])
        shifted = jnp.concatenate([head, r[V7X_SUBLANES:]], axis=0)
        xc = xc + cw_ref[CONV_W - 1 - k:CONV_W - k, :] * shifted
    tail_ref[...] = u[tt - V7X_SUBLANES:, :]

    g = jnp.dot(xc.astype(BF16), gw_ref[...], preferred_element_type=F32) + gb_ref[...]
    r_gate = jax.nn.sigmoid(g[:, :d_rnn])
    i_gate = jax.nn.sigmoid(g[:, d_rnn:])
    log_a = -RG_C * r_gate * _softplus(-lam_ref[...])
    a = jnp.exp(log_a)
    one_minus_a2 = -jnp.tanh(log_a) * (a * a + 1.0)
    b = jnp.sqrt(one_minus_a2) * (i_gate * xc)

    chunk = min(SCAN_CHUNK, tt)
    chunk_row = lax.broadcasted_iota(jnp.int32, (chunk, d_rnn), 0)
    h_in = hc_ref[...]
    pieces = []
    for c in range(tt // chunk):
        big_a, big_b = _scan_linear(a[c * chunk:(c + 1) * chunk], b[c * chunk:(c + 1) * chunk], chunk_row)
        piece = big_a * h_in + big_b
        h_in = piece[chunk - 1:chunk, :]
        pieces.append(piece)
    hseq = jnp.concatenate(pieces, axis=0)
    hc_ref[...] = h_in
    y_ref[0] = (hseq * jax.nn.gelu(gate)).astype(BF16)

    lrow = lax.broadcasted_iota(jnp.int32, (tt, V7X_LANES), 0)
    c = _cumsum_rows(lf_ref[0], lrow) + cc_ref[...]
    cc_ref[...] = c[tt - 1:tt, :]
    c_ref[0] = c
    ct_ref[0] = c.T[:n_heads, :]


def _rglru(ug, lf, conv_w, conv_b, gate_w_dense, gate_b, lam, n_heads):
    bsz, s, two_d = ug.shape
    d_rnn = two_d // 2
    tt = min(TIME_TILE, s)
    blk = lambda b, i: (b, i, 0)
    return pl.pallas_call(
        functools.partial(_rglru_kernel, d_rnn=d_rnn, n_heads=n_heads),
        out_shape=(jax.ShapeDtypeStruct((bsz, s, d_rnn), BF16),
                   jax.ShapeDtypeStruct((bsz, s, V7X_LANES), F32),
                   jax.ShapeDtypeStruct((bsz, n_heads, s), F32)),
        grid=(bsz, s // tt),
        in_specs=[pl.BlockSpec((1, tt, two_d), blk),
                  pl.BlockSpec((1, tt, V7X_LANES), blk),
                  _resident(conv_w.shape),
                  _resident((1, d_rnn)),
                  _resident(gate_w_dense.shape),
                  _resident((1, 2 * d_rnn)),
                  _resident((1, d_rnn))],
        out_specs=(pl.BlockSpec((1, tt, d_rnn), blk),
                   pl.BlockSpec((1, tt, V7X_LANES), blk),
                   pl.BlockSpec((1, n_heads, tt), lambda b, i: (b, 0, i))),
        scratch_shapes=[pltpu.VMEM((V7X_SUBLANES, d_rnn), F32),
                        pltpu.VMEM((1, d_rnn), F32),
                        pltpu.VMEM((1, V7X_LANES), F32)],
        compiler_params=_params(("parallel", "arbitrary")),
        name="rglru",
    )(ug, lf, conv_w, conv_b.reshape(1, d_rnn), gate_w_dense,
      gate_b.reshape(1, 2 * d_rnn), lam.reshape(1, d_rnn))


def _qk(qm, kj):
    return lax.dot_general(qm, kj, (((1,), (1,)), ((), ())), preferred_element_type=F32)


def _max_key_norms(k_ref, kmax_ref, chunk):
    s = k_ref.shape[1]
    first = lax.broadcasted_iota(jnp.int32, (chunk, V7X_LANES), 1) < HEAD_DIM
    first_row = lax.broadcasted_iota(jnp.int32, (1, V7X_LANES), 1) < HEAD_DIM
    m0 = jnp.zeros((1, 1), F32)
    m1 = jnp.zeros((1, 1), F32)
    for c in range(s // chunk):
        kk = k_ref[0, c * chunk:(c + 1) * chunk, :].astype(F32)
        sq = kk * kk
        n0 = jnp.sum(jnp.where(first, sq, 0.0), axis=-1, keepdims=True)
        n1 = jnp.sum(jnp.where(first, 0.0, sq), axis=-1, keepdims=True)
        m0 = jnp.maximum(m0, jnp.max(n0, axis=0, keepdims=True))
        m1 = jnp.maximum(m1, jnp.max(n1, axis=0, keepdims=True))
    kmax_ref[...] = jnp.sqrt(jnp.where(first_row, m0, m1))


def _logit_bound(qm, kmax_row, head_mask_row):
    qf = qm.astype(F32)
    qn = jnp.sqrt(jnp.sum(qf * qf, axis=-1, keepdims=True))
    km = jnp.max(jnp.where(head_mask_row, kmax_row, 0.0), axis=-1, keepdims=True)
    return qn * km * 1.01 + 1e-3


def _head_masks(rows):
    lane = lax.broadcasted_iota(jnp.int32, (rows, V7X_LANES), 1)
    lane_row = lax.broadcasted_iota(jnp.int32, (1, V7X_LANES), 1)
    full = [(lane >= hd * HEAD_DIM) & (lane < (hd + 1) * HEAD_DIM) for hd in range(HEADS_PER_TILE)]
    one = [(lane_row >= hd * HEAD_DIM) & (lane_row < (hd + 1) * HEAD_DIM)
           for hd in range(HEADS_PER_TILE)]
    return lane, full, one


def _sb_kernel(q_ref, k_ref, v_ref, o_ref, kmax_ref, *, tile):
    qi = pl.program_id(2)
    n_sub = q_ref.shape[1] // tile
    tq = tk = tile

    @pl.when(qi == 0)
    def _():
        _max_key_norms(k_ref, kmax_ref, min(1024, k_ref.shape[1]))

    _, head_masks, head_rows = _head_masks(tq)
    rows = lax.broadcasted_iota(jnp.int32, (tq, tk), 0)
    cols = lax.broadcasted_iota(jnp.int32, (tq, tk), 1)
    strictly_past = cols < rows
    tri = jnp.where(rows >= cols, 1.0, 0.0).astype(BF16)
    kmax_row = kmax_ref[...]

    def block(qneg, j, run, acc, diag):
        start = pl.multiple_of(j * tk, tk)
        kj = k_ref[0, pl.ds(start, tk), :]
        vj = v_ref[0, pl.ds(start, tk), :]
        nz = _qk(qneg, kj)
        if diag:
            nz = jnp.where(strictly_past, nz, -NEG_BIG)
        log_nb = _log_sigmoid(nz)
        incl = jnp.dot(log_nb.astype(BF16), tri, preferred_element_type=F32)
        w = jnp.exp(incl - nz if run is None else incl + run - nz)
        acc = acc + jnp.dot(w.astype(BF16), vj, preferred_element_type=F32)
        total = incl[:, 0:1]
        return (total if run is None else run + total), acc

    state = []
    for sub in range(n_sub):
        blk = qi * n_sub + sub
        q = q_ref[0, sub * tile:(sub + 1) * tile, :]
        left = jnp.maximum(blk - 1, 0)
        left_offset = jnp.where(blk >= 1, 0.0, NEG_BIG)
        for hd in range(HEADS_PER_TILE):
            qm = jnp.where(head_masks[hd], -q, jnp.zeros_like(q))
            bound = _logit_bound(qm, kmax_row, head_rows[hd])
            run, acc = block(qm, blk, None, jnp.zeros((tq, V7X_LANES), F32), True)
            run, acc = block(qm, left, run + left_offset, acc, False)
            state.append((blk, qm, bound, run, acc, jnp.max(run + bound) > EXP_ZERO_BELOW))

    results = []
    for blk, qm, bound, run, acc, go in state:

        def cond(carry):
            return jnp.logical_and(carry[0] >= 0, carry[1])

        def body(carry, qm=qm, bound=bound):
            j, _, run, acc = carry
            run, acc = block(qm, j, run, acc, False)
            return j - 1, jnp.max(run + bound) > EXP_ZERO_BELOW, run, acc

        results.append(lax.while_loop(cond, body, (blk - 2, go, run, acc))[3])

    for sub in range(n_sub):
        res = results[sub * HEADS_PER_TILE]
        for hd in range(1, HEADS_PER_TILE):
            res = jnp.where(head_masks[hd], results[sub * HEADS_PER_TILE + hd], res)
        o_ref[0, sub * tile:(sub + 1) * tile, :] = res.astype(BF16)


def _sb_attention(q, k, v):
    bsz, s, w = q.shape
    tile = min(SB_TILE, s)
    tq = min(SB_TILE * SB_SUBTILES, s)
    n_tiles = w // V7X_LANES
    return pl.pallas_call(
        functools.partial(_sb_kernel, tile=tile),
        out_shape=jax.ShapeDtypeStruct((bsz, s, w), BF16),
        grid=(bsz, n_tiles, s // tq),
        in_specs=[pl.BlockSpec((1, tq, V7X_LANES), lambda b, p, i: (b, i, p)),
                  pl.BlockSpec((1, s, V7X_LANES), lambda b, p, i: (b, 0, p)),
                  pl.BlockSpec((1, s, V7X_LANES), lambda b, p, i: (b, 0, p))],
        out_specs=pl.BlockSpec((1, tq, V7X_LANES), lambda b, p, i: (b, i, p)),
        scratch_shapes=[pltpu.VMEM((1, V7X_LANES), F32)],
        compiler_params=_params(("parallel", "parallel", "arbitrary")),
        name="sb_attn",
    )(q, k, v)


def _fox_kernel(q_ref, k_ref, v_ref, c_ref, ct_ref, o_ref, kmax_ref, *, tile):
    lane_tile = pl.program_id(1)
    qi = pl.program_id(2)
    n_sub = q_ref.shape[1] // tile
    tq = tk = tile

    @pl.when(qi == 0)
    def _():
        _max_key_norms(k_ref, kmax_ref, min(1024, k_ref.shape[1]))

    lane, head_masks, head_rows = _head_masks(tq)
    _, key_masks, _ = _head_masks(tk)
    rows = lax.broadcasted_iota(jnp.int32, (tq, tk), 0)
    cols = lax.broadcasted_iota(jnp.int32, (tq, tk), 1)
    causal = cols <= rows
    kmax_row = kmax_ref[...]

    def load(head, j):
        start = pl.multiple_of(j * tk, tk)
        kj = k_ref[0, pl.ds(start, tk), :]
        vj = v_ref[0, pl.ds(start, tk), :]
        ck = ct_ref[0, pl.ds(head, 1), pl.ds(start, tk)]
        return kj, vj, ck

    units = []
    for sub in range(n_sub):
        q = q_ref[0, sub * tile:(sub + 1) * tile, :]
        c_blk = c_ref[0, sub * tile:(sub + 1) * tile, :]
        for hd in range(HEADS_PER_TILE):
            head = lane_tile * HEADS_PER_TILE + hd
            qm = jnp.where(head_masks[hd], q, jnp.zeros_like(q))
            bound = _logit_bound(qm, kmax_row, head_rows[hd])
            cq = jnp.sum(jnp.where(lane == head, c_blk, 0.0), axis=-1, keepdims=True)
            units.append((qi * n_sub + sub, hd, head, qm, bound, cq))

    def merge(outs):
        tiles = []
        for sub in range(n_sub):
            res = outs[sub * HEADS_PER_TILE]
            for hd in range(1, HEADS_PER_TILE):
                res = jnp.where(head_masks[hd], outs[sub * HEADS_PER_TILE + hd], res)
            tiles.append(res)
        return jnp.concatenate(tiles, axis=0)

    def fixed_shift():
        def block(hd, head, qm, shift, cq_top, j, acc, diag):
            kj, vj, ck = load(head, j)
            s = _qk(qm, kj) + shift - ck
            if diag:
                s = jnp.where(causal, s, NEG_BIG)
            p = jnp.exp(s)
            v1 = jnp.where(key_masks[hd], vj, jnp.ones_like(vj))
            acc = acc + jnp.dot(p.astype(BF16), v1, preferred_element_type=F32)
            return acc, cq_top - jnp.max(ck) > EXP_ZERO_BELOW

        state = []
        for blk, hd, head, qm, bound, cq in units:
            left = jnp.maximum(blk - 1, 0)
            left_offset = jnp.where(blk >= 1, 0.0, NEG_BIG)
            shift = cq - bound
            cq_top = jnp.max(cq)
            acc, _ = block(hd, head, qm, shift, cq_top, blk, jnp.zeros((tq, V7X_LANES), F32), True)
            acc, go = block(hd, head, qm, shift + left_offset, cq_top, left, acc, False)
            state.append((shift, cq_top, acc, go))

        outs = []
        for (blk, hd, head, qm, _, _), (shift, cq_top, acc, go) in zip(units, state):

            def cond(carry):
                return jnp.logical_and(carry[0] >= 0, carry[1])

            def body(carry, hd=hd, head=head, qm=qm, shift=shift, cq_top=cq_top):
                j, _, acc = carry
                acc, go = block(hd, head, qm, shift, cq_top, j, acc, False)
                return j - 1, go, acc

            _, _, acc = lax.while_loop(cond, body, (blk - 2, go, acc))
            outs.append(acc / pltpu.roll(acc, HEAD_DIM, 1))
        return merge(outs)

    def running_shift():
        outs = []
        for blk, _, head, qm, bound, cq in units:

            def block(j, m, l, acc, diag, head=head, qm=qm, bound=bound, cq=cq):
                kj, vj, ck = load(head, j)
                s = _qk(qm, kj) + cq - ck
                if diag:
                    s = jnp.where(causal, s, NEG_BIG)
                m_new = jnp.maximum(m, jnp.max(s, axis=-1, keepdims=True))
                alpha = jnp.exp(m - m_new)
                p = jnp.exp(s - m_new)
                l = alpha * l + jnp.sum(p, axis=-1, keepdims=True)
                acc = alpha * acc + jnp.dot(p.astype(BF16), vj, preferred_element_type=F32)
                reach = cq - jnp.max(ck, axis=-1, keepdims=True)
                return m_new, l, acc, jnp.max(bound + reach - m_new) > EXP_ZERO_BELOW

            m, l, acc, go = block(blk, jnp.full((tq, 1), NEG_BIG, F32), jnp.zeros((tq, 1), F32),
                                  jnp.zeros((tq, V7X_LANES), F32), True)

            def cond(carry):
                return jnp.logical_and(carry[0] >= 0, carry[1])

            def body(carry, block=block):
                j, _, m, l, acc = carry
                m, l, acc, go = block(j, m, l, acc, False)
                return j - 1, go, m, l, acc

            _, _, _, l, acc = lax.while_loop(cond, body, (blk - 1, go, m, l, acc))
            outs.append(acc / l)
        return merge(outs)

    largest_bound = functools.reduce(jnp.maximum, [jnp.max(u[4]) for u in units])
    res = lax.cond(largest_bound <= FIXED_SHIFT_MAX_BOUND, fixed_shift, running_shift)
    o_ref[0] = res.astype(BF16)


def _fox_attention(q, k, v, c, ct):
    bsz, s, w = q.shape
    n_heads = ct.shape[1]
    tile = min(FOX_TILE, s)
    tq = min(FOX_TILE * FOX_SUBTILES, s)
    n_tiles = w // V7X_LANES
    return pl.pallas_call(
        functools.partial(_fox_kernel, tile=tile),
        out_shape=jax.ShapeDtypeStruct((bsz, s, w), BF16),
        grid=(bsz, n_tiles, s // tq),
        in_specs=[pl.BlockSpec((1, tq, V7X_LANES), lambda b, p, i: (b, i, p)),
                  pl.BlockSpec((1, s, V7X_LANES), lambda b, p, i: (b, 0, p)),
                  pl.BlockSpec((1, s, V7X_LANES), lambda b, p, i: (b, 0, p)),
                  pl.BlockSpec((1, tq, V7X_LANES), lambda b, p, i: (b, i, 0)),
                  pl.BlockSpec((1, n_heads, s), lambda b, p, i: (b, 0, 0))],
        out_specs=pl.BlockSpec((1, tq, V7X_LANES), lambda b, p, i: (b, i, p)),
        scratch_shapes=[pltpu.VMEM((1, V7X_LANES), F32)],
        compiler_params=_params(("parallel", "parallel", "arbitrary")),
        name="fox_attn",
    )(q, k, v, c, ct)


def _sbin_kernel(x_ref, g_ref, w_ref, q_ref, k_ref, v_ref, *, width):
    x = x_ref[...]
    h = _rmsnorm_rows(x, g_ref[...]).astype(BF16)
    p = jnp.dot(h, w_ref[...], preferred_element_type=F32)
    q_ref[...] = (p[:, :width] * (HEAD_DIM ** -0.5)).astype(BF16)
    k_ref[...] = p[:, width:2 * width].astype(BF16)
    v_ref[...] = p[:, 2 * width:].astype(BF16)


def _sbin(x2, g, w_qkv):
    t, d = x2.shape
    width = w_qkv.shape[1] // 3
    tm = min(TOKEN_TILE, t)
    row = lambda i: (i, 0)
    return pl.pallas_call(
        functools.partial(_sbin_kernel, width=width),
        out_shape=tuple(jax.ShapeDtypeStruct((t, width), BF16) for _ in range(3)),
        grid=(t // tm,),
        in_specs=[pl.BlockSpec((tm, d), row), _resident((1, d)), _resident(w_qkv.shape)],
        out_specs=tuple(pl.BlockSpec((tm, width), row) for _ in range(3)),
        compiler_params=_params(("parallel",)),
        name="sbin",
    )(x2, g.reshape(1, d), w_qkv)


def _block_diag_gates(gate_w):
    n_gates, n_blocks, r, _ = gate_w.shape
    eye = jnp.eye(n_blocks, dtype=gate_w.dtype)
    dense = jnp.einsum('ghij,hk->ghikj', gate_w, eye).reshape(n_gates, n_blocks * r, n_blocks * r)
    return jnp.concatenate([dense[g] for g in range(n_gates)], axis=1)


def kernel(x, ffn_norm, ffn_w_in, ffn_w_out, mix_norm, hy_w_in, rg_conv_w, rg_conv_b, rg_gate_w,
           rg_gate_b, rg_lambda, fox_fgate_b, fox_qk_norm, hy_w_out, sb_w_qkv, sb_w_out):
    bsz, s, d = x.shape
    depth = ffn_norm.shape[0]
    d_rnn = rg_conv_w.shape[2]
    n_fox = fox_fgate_b.shape[1]
    fox_w = n_fox * HEAD_DIM
    t = bsz * s

    x2 = x.reshape(t, d)
    for layer in range(depth):
        x2 = _ffn(x2, ffn_norm[layer, 0], ffn_w_in[layer, 0].astype(BF16),
                  ffn_w_out[layer, 0].astype(BF16))
        if layer % 2 == 0:
            e = layer // 2
            pad = V7X_LANES - n_fox
            w_pad = jnp.pad(hy_w_in[e], ((0, 0), (0, pad))).astype(BF16)
            fb_pad = jnp.pad(fox_fgate_b[e], (0, pad)).reshape(1, V7X_LANES)
            qk_gain = jnp.tile(fox_qk_norm[e], (1, HEADS_PER_TILE))
            ug, q, k, v, lf = _hyin(x2, mix_norm[layer], w_pad, qk_gain, fb_pad, d_rnn, fox_w)
            y_rnn, c, ct = _rglru(ug.reshape(bsz, s, 2 * d_rnn), lf.reshape(bsz, s, V7X_LANES),
                                  rg_conv_w[e], rg_conv_b[e],
                                  _block_diag_gates(rg_gate_w[e]).astype(BF16),
                                  rg_gate_b[e].reshape(-1), rg_lambda[e], n_fox)
            y_fox = _fox_attention(q.reshape(bsz, s, fox_w), k.reshape(bsz, s, fox_w),
                                   v.reshape(bsz, s, fox_w), c, ct)
            w_out = hy_w_out[e].astype(BF16)
            mixer_out = [(y_rnn.reshape(t, d_rnn), w_out[:d_rnn]),
                         (y_fox.reshape(t, fox_w), w_out[d_rnn:])]
        else:
            o = layer // 2
            q, k, v = _sbin(x2, mix_norm[layer], sb_w_qkv[o].astype(BF16))
            width = q.shape[1]
            y = _sb_attention(q.reshape(bsz, s, width), k.reshape(bsz, s, width),
                              v.reshape(bsz, s, width))
            mixer_out = [(y.reshape(t, width), sb_w_out[o].astype(BF16))]
        x2 = _ffn(x2, ffn_norm[layer, 1], ffn_w_in[layer, 1].astype(BF16),
                  ffn_w_out[layer, 1].astype(BF16), mixer_out)
    return x2.reshape(bsz, s, d)
```

```python
import functools

import jax
import jax.numpy as jnp
from jax import lax
from jax.experimental import pallas as pl
from jax.experimental.pallas import tpu as pltpu

F32 = jnp.float32
BF16 = jnp.bfloat16

HEAD_DIM = 64
RMS_EPS = 1e-6
RG_C = 8.0
CONV_W = 4

V7X_LANES = 128
V7X_SUBLANES = 8
V7X_VMEM_LIMIT_BYTES = 56 * 1024 * 1024
HEADS_PER_TILE = V7X_LANES // HEAD_DIM

EXP_ZERO_BELOW = -104.0
NEG_BIG = -1e30
LOG2E = 1.4426950408889634
FIXED_SHIFT_MAX_BOUND = 30.0

TOKEN_TILE = 512
FFN_TILE = 1024
TIME_TILE = 512
SCAN_CHUNK = 8
SB_TILE = 256
SB_SUBTILES = 2
FOX_TILE = 512
FOX_SUBTILES = 2
FOX_STRAIGHT_BLOCKS = 3


def _params(semantics):
    return pltpu.CompilerParams(dimension_semantics=semantics,
                                vmem_limit_bytes=V7X_VMEM_LIMIT_BYTES)


def _resident(shape):
    nd = len(shape)
    return pl.BlockSpec(shape, lambda *_: (0,) * nd, pipeline_mode=pl.Buffered(1))


def _rmsnorm_rows(x, g):
    ms = jnp.mean(x * x, axis=-1, keepdims=True)
    return x * lax.rsqrt(ms + RMS_EPS) * g


def _softplus(x):
    return jnp.maximum(x, 0.0) + jnp.log(1.0 + jnp.exp(-jnp.abs(x)))


def _sigmoid(x):
    return 1.0 / (1.0 + jnp.exp(-x))


def _log_sigmoid(x):
    return jnp.minimum(x, 0.0) - jnp.log(1.0 + jnp.exp2(jnp.abs(x) * (-LOG2E)))


def _ffn_kernel(*refs, d_ff, n_parts, sub_rows):
    x_ref, g_ref, win_ref, wout_ref = refs[:4]
    o_ref = refs[-1]
    for r in range(x_ref.shape[0] // sub_rows):
        rows = slice(r * sub_rows, (r + 1) * sub_rows)
        x = x_ref[rows, :]
        for i in range(n_parts):
            x = x + jnp.dot(refs[4 + 2 * i][rows, :], refs[5 + 2 * i][...], preferred_element_type=F32)
        h = _rmsnorm_rows(x, g_ref[...]).astype(BF16)
        ab = jnp.dot(h, win_ref[...], preferred_element_type=F32)
        a = ab[:, :d_ff]
        b = ab[:, d_ff:]
        t = (a * _sigmoid(a) * b).astype(BF16)
        y = jnp.dot(t, wout_ref[...], preferred_element_type=F32)
        o_ref[rows, :] = x + 0.5 * y


def _ffn(x2, g, w_in, w_out, parts=()):
    t, d = x2.shape
    d_ff = w_out.shape[0]
    tm = min(FFN_TILE, t)
    row = lambda i: (i, 0)
    in_specs = [pl.BlockSpec((tm, d), row), _resident((1, d)), _resident(w_in.shape),
                _resident(w_out.shape)]
    args = [x2, g.reshape(1, d), w_in, w_out]
    for y, w in parts:
        in_specs += [pl.BlockSpec((tm, y.shape[1]), row), _resident(w.shape)]
        args += [y, w]
    return pl.pallas_call(
        functools.partial(_ffn_kernel, d_ff=d_ff, n_parts=len(parts), sub_rows=min(TOKEN_TILE, tm)),
        out_shape=jax.ShapeDtypeStruct((t, d), F32),
        grid=(t // tm,),
        in_specs=in_specs,
        out_specs=pl.BlockSpec((tm, d), row),
        compiler_params=_params(("parallel",)),
        name="ffn",
    )(*args)


def _head_rmsnorm(t, gain_row, first_head):
    outs = []
    for j in range(t.shape[1] // V7X_LANES):
        tj = t[:, j * V7X_LANES:(j + 1) * V7X_LANES]
        sq = tj * tj
        s0 = jnp.sum(jnp.where(first_head, sq, 0.0), axis=-1, keepdims=True)
        s1 = jnp.sum(jnp.where(first_head, 0.0, sq), axis=-1, keepdims=True)
        ms = jnp.where(first_head, s0, s1) * (1.0 / HEAD_DIM)
        outs.append(tj * lax.rsqrt(ms + RMS_EPS) * gain_row)
    return jnp.concatenate(outs, axis=-1)


def _hyin_kernel(x_ref, g_ref, w_ref, qkg_ref, fb_ref,
                 ug_ref, q_ref, k_ref, v_ref, lf_ref, *, d_rnn, fox_w):
    x = x_ref[...]
    h = _rmsnorm_rows(x, g_ref[...]).astype(BF16)
    p = jnp.dot(h, w_ref[...], preferred_element_type=F32)
    o = 2 * d_rnn
    ug_ref[...] = p[:, :o]
    first_head = lax.broadcasted_iota(jnp.int32, (x.shape[0], V7X_LANES), 1) < HEAD_DIM
    scale = HEAD_DIM ** -0.5
    q = _head_rmsnorm(p[:, o:o + fox_w], qkg_ref[0:1, :], first_head)
    q_ref[...] = (q * scale).astype(BF16)
    k = _head_rmsnorm(p[:, o + fox_w:o + 2 * fox_w], qkg_ref[1:2, :], first_head)
    k_ref[...] = k.astype(BF16)
    v_ref[...] = p[:, o + 2 * fox_w:o + 3 * fox_w].astype(BF16)
    f = p[:, o + 3 * fox_w:] + fb_ref[...]
    lf_ref[...] = _log_sigmoid(f)


def _hyin(x2, g, w_pad, qk_gain, fb_pad, d_rnn, fox_w):
    t, d = x2.shape
    tm = min(TOKEN_TILE, t)
    row = lambda i: (i, 0)
    return pl.pallas_call(
        functools.partial(_hyin_kernel, d_rnn=d_rnn, fox_w=fox_w),
        out_shape=(jax.ShapeDtypeStruct((t, 2 * d_rnn), F32),
                   jax.ShapeDtypeStruct((t, fox_w), BF16),
                   jax.ShapeDtypeStruct((t, fox_w), BF16),
                   jax.ShapeDtypeStruct((t, fox_w), BF16),
                   jax.ShapeDtypeStruct((t, V7X_LANES), F32)),
        grid=(t // tm,),
        in_specs=[pl.BlockSpec((tm, d), row),
                  _resident((1, d)),
                  _resident(w_pad.shape),
                  _resident(qk_gain.shape),
                  _resident(fb_pad.shape)],
        out_specs=(pl.BlockSpec((tm, 2 * d_rnn), row),
                   pl.BlockSpec((tm, fox_w), row),
                   pl.BlockSpec((tm, fox_w), row),
                   pl.BlockSpec((tm, fox_w), row),
                   pl.BlockSpec((tm, V7X_LANES), row)),
        compiler_params=_params(("parallel",)),
        name="hyin",
    )(x2, g.reshape(1, d), w_pad, qk_gain, fb_pad)


def _scan_linear(a, b, row):
    n = a.shape[0]
    s = 1
    while s < n:
        keep = row >= s
        b = jnp.where(keep, a * pltpu.roll(b, s, 0) + b, b)
        a = jnp.where(keep, a * pltpu.roll(a, s, 0), a)
        s *= 2
    return a, b


def _cumsum_rows(x, row):
    n = x.shape[0]
    s = 1
    while s < n:
        x = jnp.where(row >= s, x + pltpu.roll(x, s, 0), x)
        s *= 2
    return x


def _rglru_kernel(ug_ref, lf_ref, cw_ref, cb_ref, gw_ref, gb_ref, lam_ref,
                  y_ref, c_ref, ct_ref, tail_ref, hc_ref, cc_ref, *, d_rnn, n_heads):
    @pl.when(pl.program_id(1) == 0)
    def _():
        tail_ref[...] = jnp.zeros_like(tail_ref)
        hc_ref[...] = jnp.zeros_like(hc_ref)
        cc_ref[...] = jnp.zeros_like(cc_ref)

    u = ug_ref[0, :, :d_rnn]
    gate = ug_ref[0, :, d_rnn:]
    tt = u.shape[0]
    row = lax.broadcasted_iota(jnp.int32, (tt, d_rnn), 0)

    tail = tail_ref[...]
    head_row = lax.broadcasted_iota(jnp.int32, (V7X_SUBLANES, d_rnn), 0)
    xc = cb_ref[...] + cw_ref[CONV_W - 1:CONV_W, :] * u
    for k in range(1, CONV_W):
        r = pltpu.roll(u, k, 0)
        head = jnp.where(head_row < k, pltpu.roll(tail, k, 0), r[:V7X_SUBLANES])
        shifted = jnp.concatenate([head, r[V7X_SUBLANES:]], axis=0)
        xc = xc + cw_ref[CONV_W - 1 - k:CONV_W - k, :] * shifted
    tail_ref[...] = u[tt - V7X_SUBLANES:, :]

    g = jnp.dot(xc.astype(BF16), gw_ref[...], preferred_element_type=F32) + gb_ref[...]
    r_gate = jax.nn.sigmoid(g[:, :d_rnn])
    i_gate = jax.nn.sigmoid(g[:, d_rnn:])
    log_a = -RG_C * r_gate * _softplus(-lam_ref[...])
    a = jnp.exp(log_a)
    one_minus_a2 = -jnp.tanh(log_a) * (a * a + 1.0)
    b = jnp.sqrt(one_minus_a2) * (i_gate * xc)

    chunk = min(SCAN_CHUNK, tt)
    chunk_row = lax.broadcasted_iota(jnp.int32, (chunk, d_rnn), 0)
    h_in = hc_ref[...]
    pieces = []
    for c in range(tt // chunk):
        big_a, big_b = _scan_linear(a[c * chunk:(c + 1) * chunk], b[c * chunk:(c + 1) * chunk], chunk_row)
        piece = big_a * h_in + big_b
        h_in = piece[chunk - 1:chunk, :]
        pieces.append(piece)
    hseq = jnp.concatenate(pieces, axis=0)
    hc_ref[...] = h_in
    y_ref[0] = (hseq * jax.nn.gelu(gate)).astype(BF16)

    lrow = lax.broadcasted_iota(jnp.int32, (tt, V7X_LANES), 0)
    c = _cumsum_rows(lf_ref[0], lrow) + cc_ref[...]
    cc_ref[...] = c[tt - 1:tt, :]
    c_ref[0] = c
    ct_ref[0] = c.T[:n_heads, :]


def _rglru(ug, lf, conv_w, conv_b, gate_w_dense, gate_b, lam, n_heads):
    bsz, s, two_d = ug.shape
    d_rnn = two_d // 2
    tt = min(TIME_TILE, s)
    blk = lambda b, i: (b, i, 0)
    return pl.pallas_call(
        functools.partial(_rglru_kernel, d_rnn=d_rnn, n_heads=n_heads),
        out_shape=(jax.ShapeDtypeStruct((bsz, s, d_rnn), BF16),
                   jax.ShapeDtypeStruct((bsz, s, V7X_LANES), F32),
                   jax.ShapeDtypeStruct((bsz, n_heads, s), F32)),
        grid=(bsz, s // tt),
        in_specs=[pl.BlockSpec((1, tt, two_d), blk),
                  pl.BlockSpec((1, tt, V7X_LANES), blk),
                  _resident(conv_w.shape),
                  _resident((1, d_rnn)),
                  _resident(gate_w_dense.shape),
                  _resident((1, 2 * d_rnn)),
                  _resident((1, d_rnn))],
        out_specs=(pl.BlockSpec((1, tt, d_rnn), blk),
                   pl.BlockSpec((1, tt, V7X_LANES), blk),
                   pl.BlockSpec((1, n_heads, tt), lambda b, i: (b, 0, i))),
        scratch_shapes=[pltpu.VMEM((V7X_SUBLANES, d_rnn), F32),
                        pltpu.VMEM((1, d_rnn), F32),
                        pltpu.VMEM((1, V7X_LANES), F32)],
        compiler_params=_params(("parallel", "arbitrary")),
        name="rglru",
    )(ug, lf, conv_w, conv_b.reshape(1, d_rnn), gate_w_dense,
      gate_b.reshape(1, 2 * d_rnn), lam.reshape(1, d_rnn))


def _qk(qm, kj):
    return lax.dot_general(qm, kj, (((1,), (1,)), ((), ())), preferred_element_type=F32)


def _max_key_norms(k_ref, kmax_ref, chunk):
    s = k_ref.shape[1]
    first = lax.broadcasted_iota(jnp.int32, (chunk, V7X_LANES), 1) < HEAD_DIM
    first_row = lax.broadcasted_iota(jnp.int32, (1, V7X_LANES), 1) < HEAD_DIM
    m0 = jnp.zeros((1, 1), F32)
    m1 = jnp.zeros((1, 1), F32)
    for c in range(s // chunk):
        kk = k_ref[0, c * chunk:(c + 1) * chunk, :].astype(F32)
        sq = kk * kk
        n0 = jnp.sum(jnp.where(first, sq, 0.0), axis=-1, keepdims=True)
        n1 = jnp.sum(jnp.where(first, 0.0, sq), axis=-1, keepdims=True)
        m0 = jnp.maximum(m0, jnp.max(n0, axis=0, keepdims=True))
        m1 = jnp.maximum(m1, jnp.max(n1, axis=0, keepdims=True))
    kmax_ref[...] = jnp.sqrt(jnp.where(first_row, m0, m1))


def _logit_bound(qm, kmax_row, head_mask_row):
    qf = qm.astype(F32)
    qn = jnp.sqrt(jnp.sum(qf * qf, axis=-1, keepdims=True))
    km = jnp.max(jnp.where(head_mask_row, kmax_row, 0.0), axis=-1, keepdims=True)
    return qn * km * 1.01 + 1e-3


def _head_masks(rows):
    lane = lax.broadcasted_iota(jnp.int32, (rows, V7X_LANES), 1)
    lane_row = lax.broadcasted_iota(jnp.int32, (1, V7X_LANES), 1)
    full = [(lane >= hd * HEAD_DIM) & (lane < (hd + 1) * HEAD_DIM) for hd in range(HEADS_PER_TILE)]
    one = [(lane_row >= hd * HEAD_DIM) & (lane_row < (hd + 1) * HEAD_DIM)
           for hd in range(HEADS_PER_TILE)]
    return lane, full, one


def _sb_kernel(q_ref, k_ref, v_ref, o_ref, kmax_ref, *, tile):
    qi = pl.program_id(2)
    n_sub = q_ref.shape[1] // tile
    tq = tk = tile

    @pl.when(qi == 0)
    def _():
        _max_key_norms(k_ref, kmax_ref, min(1024, k_ref.shape[1]))

    _, head_masks, head_rows = _head_masks(tq)
    rows = lax.broadcasted_iota(jnp.int32, (tq, tk), 0)
    cols = lax.broadcasted_iota(jnp.int32, (tq, tk), 1)
    strictly_past = cols < rows
    tri = jnp.where(rows >= cols, 1.0, 0.0).astype(BF16)
    kmax_row = kmax_ref[...]

    def block(qneg, j, run, acc, diag):
        start = pl.multiple_of(j * tk, tk)
        kj = k_ref[0, pl.ds(start, tk), :]
        vj = v_ref[0, pl.ds(start, tk), :]
        nz = _qk(qneg, kj)
        if diag:
            nz = jnp.where(strictly_past, nz, -NEG_BIG)
        log_nb = _log_sigmoid(nz)
        incl = jnp.dot(log_nb.astype(BF16), tri, preferred_element_type=F32)
        w = jnp.exp(incl - nz if run is None else incl + run - nz)
        acc = acc + jnp.dot(w.astype(BF16), vj, preferred_element_type=F32)
        total = incl[:, 0:1]
        return (total if run is None else run + total), acc

    state = []
    for sub in range(n_sub):
        blk = qi * n_sub + sub
        q = q_ref[0, sub * tile:(sub + 1) * tile, :]
        left = jnp.maximum(blk - 1, 0)
        left_offset = jnp.where(blk >= 1, 0.0, NEG_BIG)
        for hd in range(HEADS_PER_TILE):
            qm = jnp.where(head_masks[hd], -q, jnp.zeros_like(q))
            bound = _logit_bound(qm, kmax_row, head_rows[hd])
            run, acc = block(qm, blk, None, jnp.zeros((tq, V7X_LANES), F32), True)
            run, acc = block(qm, left, run + left_offset, acc, False)
            state.append((blk, qm, bound, run, acc, jnp.max(run + bound) > EXP_ZERO_BELOW))

    results = []
    for blk, qm, bound, run, acc, go in state:

        def cond(carry):
            return jnp.logical_and(carry[0] >= 0, carry[1])

        def body(carry, qm=qm, bound=bound):
            j, _, run, acc = carry
            run, acc = block(qm, j, run, acc, False)
            return j - 1, jnp.max(run + bound) > EXP_ZERO_BELOW, run, acc

        results.append(lax.while_loop(cond, body, (blk - 2, go, run, acc))[3])

    for sub in range(n_sub):
        res = results[sub * HEADS_PER_TILE]
        for hd in range(1, HEADS_PER_TILE):
            res = jnp.where(head_masks[hd], results[sub * HEADS_PER_TILE + hd], res)
        o_ref[0, sub * tile:(sub + 1) * tile, :] = res.astype(BF16)


def _sb_attention(q, k, v):
    bsz, s, w = q.shape
    tile = min(SB_TILE, s)
    tq = min(SB_TILE * SB_SUBTILES, s)
    n_tiles = w // V7X_LANES
    return pl.pallas_call(
        functools.partial(_sb_kernel, tile=tile),
        out_shape=jax.ShapeDtypeStruct((bsz, s, w), BF16),
        grid=(bsz, n_tiles, s // tq),
        in_specs=[pl.BlockSpec((1, tq, V7X_LANES), lambda b, p, i: (b, i, p)),
                  pl.BlockSpec((1, s, V7X_LANES), lambda b, p, i: (b, 0, p)),
                  pl.BlockSpec((1, s, V7X_LANES), lambda b, p, i: (b, 0, p))],
        out_specs=pl.BlockSpec((1, tq, V7X_LANES), lambda b, p, i: (b, i, p)),
        scratch_shapes=[pltpu.VMEM((1, V7X_LANES), F32)],
        compiler_params=_params(("parallel", "parallel", "arbitrary")),
        name="sb_attn",
    )(q, k, v)


def _fox_kernel(q_ref, k_ref, v_ref, c_ref, ct_ref, o_ref, kmax_ref, *, tile):
    lane_tile = pl.program_id(1)
    qi = pl.program_id(2)
    n_sub = q_ref.shape[1] // tile
    tq = tk = tile

    @pl.when(qi == 0)
    def _():
        _max_key_norms(k_ref, kmax_ref, min(1024, k_ref.shape[1]))

    lane, head_masks, head_rows = _head_masks(tq)
    _, key_masks, _ = _head_masks(tk)
    rows = lax.broadcasted_iota(jnp.int32, (tq, tk), 0)
    cols = lax.broadcasted_iota(jnp.int32, (tq, tk), 1)
    causal = cols <= rows
    kmax_row = kmax_ref[...]

    def load(head, j):
        start = pl.multiple_of(j * tk, tk)
        kj = k_ref[0, pl.ds(start, tk), :]
        vj = v_ref[0, pl.ds(start, tk), :]
        ck = ct_ref[0, pl.ds(head, 1), pl.ds(start, tk)]
        return kj, vj, ck

    units = []
    for sub in range(n_sub):
        q = q_ref[0, sub * tile:(sub + 1) * tile, :]
        c_blk = c_ref[0, sub * tile:(sub + 1) * tile, :]
        for hd in range(HEADS_PER_TILE):
            head = lane_tile * HEADS_PER_TILE + hd
            qm = jnp.where(head_masks[hd], q, jnp.zeros_like(q))
            bound = _logit_bound(qm, kmax_row, head_rows[hd])
            cq = jnp.sum(jnp.where(lane == head, c_blk, 0.0), axis=-1, keepdims=True)
            units.append((qi * n_sub + sub, hd, head, qm, bound, cq))

    def merge(outs):
        tiles = []
        for sub in range(n_sub):
            res = outs[sub * HEADS_PER_TILE]
            for hd in range(1, HEADS_PER_TILE):
                res = jnp.where(head_masks[hd], outs[sub * HEADS_PER_TILE + hd], res)
            tiles.append(res)
        return jnp.concatenate(tiles, axis=0)

    def fixed_shift():
        def block(hd, head, qm, shift, cq_top, j, acc, diag):
            kj, vj, ck = load(head, j)
            s = _qk(qm, kj) + shift - ck
            if diag:
                s = jnp.where(causal, s, NEG_BIG)
            p = jnp.exp(s)
            v1 = jnp.where(key_masks[hd], vj, jnp.ones_like(vj))
            acc = acc + jnp.dot(p.astype(BF16), v1, preferred_element_type=F32)
            return acc, cq_top - jnp.max(ck) > EXP_ZERO_BELOW

        state = []
        for blk, hd, head, qm, bound, cq in units:
            shift = cq - bound
            cq_top = jnp.max(cq)
            acc, go = block(hd, head, qm, shift, cq_top, blk, jnp.zeros((tq, V7X_LANES), F32), True)
            for back in range(1, FOX_STRAIGHT_BLOCKS):
                offset = jnp.where(blk >= back, 0.0, NEG_BIG)
                acc, go = block(hd, head, qm, shift + offset, cq_top, jnp.maximum(blk - back, 0), acc, False)
            state.append((shift, cq_top, acc, go))

        outs = []
        for (blk, hd, head, qm, _, _), (shift, cq_top, acc, go) in zip(units, state):

            def cond(carry):
                return jnp.logical_and(carry[0] >= 0, carry[1])

            def body(carry, hd=hd, head=head, qm=qm, shift=shift, cq_top=cq_top):
                j, _, acc = carry
                acc, go = block(hd, head, qm, shift, cq_top, j, acc, False)
                return j - 1, go, acc

            _, _, acc = lax.while_loop(cond, body, (blk - FOX_STRAIGHT_BLOCKS, go, acc))
            outs.append(acc / pltpu.roll(acc, HEAD_DIM, 1))
        return merge(outs)

    def running_shift():
        outs = []
        for blk, _, head, qm, bound, cq in units:

            def block(j, m, l, acc, diag, head=head, qm=qm, bound=bound, cq=cq):
                kj, vj, ck = load(head, j)
                s = _qk(qm, kj) + cq - ck
                if diag:
                    s = jnp.where(causal, s, NEG_BIG)
                m_new = jnp.maximum(m, jnp.max(s, axis=-1, keepdims=True))
                alpha = jnp.exp(m - m_new)
                p = jnp.exp(s - m_new)
                l = alpha * l + jnp.sum(p, axis=-1, keepdims=True)
                acc = alpha * acc + jnp.dot(p.astype(BF16), vj, preferred_element_type=F32)
                reach = cq - jnp.max(ck, axis=-1, keepdims=True)
                return m_new, l, acc, jnp.max(bound + reach - m_new) > EXP_ZERO_BELOW

            m, l, acc, go = block(blk, jnp.full((tq, 1), NEG_BIG, F32), jnp.zeros((tq, 1), F32),
                                  jnp.zeros((tq, V7X_LANES), F32), True)

            def cond(carry):
                return jnp.logical_and(carry[0] >= 0, carry[1])

            def body(carry, block=block):
                j, _, m, l, acc = carry
                m, l, acc, go = block(j, m, l, acc, False)
                return j - 1, go, m, l, acc

            _, _, _, l, acc = lax.while_loop(cond, body, (blk - 1, go, m, l, acc))
            outs.append(acc / l)
        return merge(outs)

    largest_bound = functools.reduce(jnp.maximum, [jnp.max(u[4]) for u in units])
    res = lax.cond(largest_bound <= FIXED_SHIFT_MAX_BOUND, fixed_shift, running_shift)
    o_ref[0] = res.astype(BF16)


def _fox_attention(q, k, v, c, ct):
    bsz, s, w = q.shape
    n_heads = ct.shape[1]
    tile = min(FOX_TILE, s)
    tq = min(FOX_TILE * FOX_SUBTILES, s)
    n_tiles = w // V7X_LANES
    return pl.pallas_call(
        functools.partial(_fox_kernel, tile=tile),
        out_shape=jax.ShapeDtypeStruct((bsz, s, w), BF16),
        grid=(bsz, n_tiles, s // tq),
        in_specs=[pl.BlockSpec((1, tq, V7X_LANES), lambda b, p, i: (b, i, p)),
                  pl.BlockSpec((1, s, V7X_LANES), lambda b, p, i: (b, 0, p)),
                  pl.BlockSpec((1, s, V7X_LANES), lambda b, p, i: (b, 0, p)),
                  pl.BlockSpec((1, tq, V7X_LANES), lambda b, p, i: (b, i, 0)),
                  pl.BlockSpec((1, n_heads, s), lambda b, p, i: (b, 0, 0))],
        out_specs=pl.BlockSpec((1, tq, V7X_LANES), lambda b, p, i: (b, i, p)),
        scratch_shapes=[pltpu.VMEM((1, V7X_LANES), F32)],
        compiler_params=_params(("parallel", "parallel", "arbitrary")),
        name="fox_attn",
    )(q, k, v, c, ct)


def _sbin_kernel(x_ref, g_ref, w_ref, q_ref, k_ref, v_ref, *, width):
    x = x_ref[...]
    h = _rmsnorm_rows(x, g_ref[...]).astype(BF16)
    p = jnp.dot(h, w_ref[...], preferred_element_type=F32)
    q_ref[...] = (p[:, :width] * (HEAD_DIM ** -0.5)).astype(BF16)
    k_ref[...] = p[:, width:2 * width].astype(BF16)
    v_ref[...] = p[:, 2 * width:].astype(BF16)


def _sbin(x2, g, w_qkv):
    t, d = x2.shape
    width = w_qkv.shape[1] // 3
    tm = min(TOKEN_TILE, t)
    row = lambda i: (i, 0)
    return pl.pallas_call(
        functools.partial(_sbin_kernel, width=width),
        out_shape=tuple(jax.ShapeDtypeStruct((t, width), BF16) for _ in range(3)),
        grid=(t // tm,),
        in_specs=[pl.BlockSpec((tm, d), row), _resident((1, d)), _resident(w_qkv.shape)],
        out_specs=tuple(pl.BlockSpec((tm, width), row) for _ in range(3)),
        compiler_params=_params(("parallel",)),
        name="sbin",
    )(x2, g.reshape(1, d), w_qkv)


def _block_diag_gates(gate_w):
    n_gates, n_blocks, r, _ = gate_w.shape
    eye = jnp.eye(n_blocks, dtype=gate_w.dtype)
    dense = jnp.einsum('ghij,hk->ghikj', gate_w, eye).reshape(n_gates, n_blocks * r, n_blocks * r)
    return jnp.concatenate([dense[g] for g in range(n_gates)], axis=1)


def kernel(x, ffn_norm, ffn_w_in, ffn_w_out, mix_norm, hy_w_in, rg_conv_w, rg_conv_b, rg_gate_w,
           rg_gate_b, rg_lambda, fox_fgate_b, fox_qk_norm, hy_w_out, sb_w_qkv, sb_w_out):
    bsz, s, d = x.shape
    depth = ffn_norm.shape[0]
    d_rnn = rg_conv_w.shape[2]
    n_fox = fox_fgate_b.shape[1]
    fox_w = n_fox * HEAD_DIM
    t = bsz * s

    x2 = x.reshape(t, d)
    for layer in range(depth):
        x2 = _ffn(x2, ffn_norm[layer, 0], ffn_w_in[layer, 0].astype(BF16),
                  ffn_w_out[layer, 0].astype(BF16))
        if layer % 2 == 0:
            e = layer // 2
            pad = V7X_LANES - n_fox
            w_pad = jnp.pad(hy_w_in[e], ((0, 0), (0, pad))).astype(BF16)
            fb_pad = jnp.pad(fox_fgate_b[e], (0, pad)).reshape(1, V7X_LANES)
            qk_gain = jnp.tile(fox_qk_norm[e], (1, HEADS_PER_TILE))
            ug, q, k, v, lf = _hyin(x2, mix_norm[layer], w_pad, qk_gain, fb_pad, d_rnn, fox_w)
            y_rnn, c, ct = _rglru(ug.reshape(bsz, s, 2 * d_rnn), lf.reshape(bsz, s, V7X_LANES),
                                  rg_conv_w[e], rg_conv_b[e],
                                  _block_diag_gates(rg_gate_w[e]).astype(BF16),
                                  rg_gate_b[e].reshape(-1), rg_lambda[e], n_fox)
            y_fox = _fox_attention(q.reshape(bsz, s, fox_w), k.reshape(bsz, s, fox_w),
                                   v.reshape(bsz, s, fox_w), c, ct)
            w_out = hy_w_out[e].astype(BF16)
            mixer_out = [(y_rnn.reshape(t, d_rnn), w_out[:d_rnn]),
                         (y_fox.reshape(t, fox_w), w_out[d_rnn:])]
        else:
            o = layer // 2
            q, k, v = _sbin(x2, mix_norm[layer], sb_w_qkv[o].astype(BF16))
            width = q.shape[1]
            y = _sb_attention(q.reshape(bsz, s, width), k.reshape(bsz, s, width),
                              v.reshape(bsz, s, width))
            mixer_out = [(y.reshape(t, width), sb_w_out[o].astype(BF16))]
        x2 = _ffn(x2, ffn_norm[layer, 1], ffn_w_in[layer, 1].astype(BF16),
                  ffn_w_out[layer, 1].astype(BF16), mixer_out)
    return x2.reshape(bsz, s, d)
```

```python
import functools

import jax
import jax.numpy as jnp
from jax import lax
from jax.experimental import pallas as pl
from jax.experimental.pallas import tpu as pltpu

F32 = jnp.float32
BF16 = jnp.bfloat16

HEAD_DIM = 64
RMS_EPS = 1e-6
RG_C = 8.0
CONV_W = 4

V7X_LANES = 128
V7X_SUBLANES = 8
V7X_VMEM_LIMIT_BYTES = 56 * 1024 * 1024
HEADS_PER_TILE = V7X_LANES // HEAD_DIM

EXP_ZERO_BELOW = -104.0
NEG_BIG = -1e30
LOG2E = 1.4426950408889634
FIXED_SHIFT_MAX_BOUND = 30.0

TOKEN_TILE = 512
FFN_TILE = 1024
TIME_TILE = 512
SCAN_CHUNK = 8
SB_TILE = 256
SB_SUBTILES = 2
FOX_TILE = 512
FOX_SUBTILES = 2
FOX_STRAIGHT_BLOCKS = 3


def _params(semantics):
    return pltpu.CompilerParams(dimension_semantics=semantics,
                                vmem_limit_bytes=V7X_VMEM_LIMIT_BYTES)


def _resident(shape):
    nd = len(shape)
    return pl.BlockSpec(shape, lambda *_: (0,) * nd, pipeline_mode=pl.Buffered(1))


def _rmsnorm_rows(x, g):
    ms = jnp.mean(x * x, axis=-1, keepdims=True)
    return x * lax.rsqrt(ms + RMS_EPS) * g


def _softplus(x):
    return jnp.maximum(x, 0.0) + jnp.log(1.0 + jnp.exp(-jnp.abs(x)))


def _sigmoid(x):
    return 1.0 / (1.0 + jnp.exp(-x))


def _log_sigmoid(x):
    return jnp.minimum(x, 0.0) - jnp.log(1.0 + jnp.exp2(jnp.abs(x) * (-LOG2E)))


def _ffn_kernel(*refs, d_ff, n_parts, sub_rows):
    x_ref, g_ref, win_ref, wout_ref = refs[:4]
    o_ref = refs[-1]
    for r in range(x_ref.shape[0] // sub_rows):
        rows = slice(r * sub_rows, (r + 1) * sub_rows)
        x = x_ref[rows, :]
        for i in range(n_parts):
            x = x + jnp.dot(refs[4 + 2 * i][rows, :], refs[5 + 2 * i][...], preferred_element_type=F32)
        h = _rmsnorm_rows(x, g_ref[...]).astype(BF16)
        ab = jnp.dot(h, win_ref[...], preferred_element_type=F32)
        a = ab[:, :d_ff]
        b = ab[:, d_ff:]
        t = (a * _sigmoid(a) * b).astype(BF16)
        y = jnp.dot(t, wout_ref[...], preferred_element_type=F32)
        o_ref[rows, :] = x + 0.5 * y


def _ffn(x2, g, w_in_all, w_out_all, which, parts=()):
    t, d = x2.shape
    d_ff = w_out_all.shape[2]
    tm = min(FFN_TILE, t)
    row = lambda i: (i, 0)
    pick = lambda i: (which[0], which[1], 0, 0)
    in_specs = [pl.BlockSpec((tm, d), row), _resident((1, d)),
                pl.BlockSpec((None, None) + w_in_all.shape[2:], pick, pipeline_mode=pl.Buffered(1)),
                pl.BlockSpec((None, None) + w_out_all.shape[2:], pick, pipeline_mode=pl.Buffered(1))]
    args = [x2, g.reshape(1, d), w_in_all, w_out_all]
    for y, w in parts:
        in_specs += [pl.BlockSpec((tm, y.shape[1]), row), _resident(w.shape)]
        args += [y, w]
    return pl.pallas_call(
        functools.partial(_ffn_kernel, d_ff=d_ff, n_parts=len(parts), sub_rows=min(TOKEN_TILE, tm)),
        out_shape=jax.ShapeDtypeStruct((t, d), F32),
        grid=(t // tm,),
        in_specs=in_specs,
        out_specs=pl.BlockSpec((tm, d), row),
        compiler_params=_params(("parallel",)),
        name="ffn",
    )(*args)


def _head_rmsnorm(t, gain_row, first_head):
    outs = []
    for j in range(t.shape[1] // V7X_LANES):
        tj = t[:, j * V7X_LANES:(j + 1) * V7X_LANES]
        sq = tj * tj
        s0 = jnp.sum(jnp.where(first_head, sq, 0.0), axis=-1, keepdims=True)
        s1 = jnp.sum(jnp.where(first_head, 0.0, sq), axis=-1, keepdims=True)
        ms = jnp.where(first_head, s0, s1) * (1.0 / HEAD_DIM)
        outs.append(tj * lax.rsqrt(ms + RMS_EPS) * gain_row)
    return jnp.concatenate(outs, axis=-1)


def _scan_linear(a, b, row):
    n = a.shape[0]
    s = 1
    while s < n:
        keep = row >= s
        b = jnp.where(keep, a * pltpu.roll(b, s, 0) + b, b)
        a = jnp.where(keep, a * pltpu.roll(a, s, 0), a)
        s *= 2
    return a, b


def _cumsum_rows(x, row):
    n = x.shape[0]
    s = 1
    while s < n:
        x = jnp.where(row >= s, x + pltpu.roll(x, s, 0), x)
        s *= 2
    return x


def _hymix_kernel(x_ref, g_ref, w_ref, qkg_ref, fb_ref, cw_ref, cb_ref, gw_ref, gb_ref, lam_ref,
                  q_ref, k_ref, v_ref, y_ref, c_ref, ct_ref, tail_ref, hc_ref, cc_ref,
                  *, d_rnn, fox_w, n_heads):
    @pl.when(pl.program_id(1) == 0)
    def _():
        tail_ref[...] = jnp.zeros_like(tail_ref)
        hc_ref[...] = jnp.zeros_like(hc_ref)
        cc_ref[...] = jnp.zeros_like(cc_ref)

    x = x_ref[0]
    tt = x.shape[0]
    h = _rmsnorm_rows(x, g_ref[...]).astype(BF16)
    p = jnp.dot(h, w_ref[...], preferred_element_type=F32)
    o = 2 * d_rnn
    first_head = lax.broadcasted_iota(jnp.int32, (tt, V7X_LANES), 1) < HEAD_DIM
    scale = HEAD_DIM ** -0.5
    q = _head_rmsnorm(p[:, o:o + fox_w], qkg_ref[0:1, :], first_head)
    q_ref[0] = (q * scale).astype(BF16)
    k = _head_rmsnorm(p[:, o + fox_w:o + 2 * fox_w], qkg_ref[1:2, :], first_head)
    k_ref[0] = k.astype(BF16)
    v_ref[0] = p[:, o + 2 * fox_w:o + 3 * fox_w].astype(BF16)
    log_f = _log_sigmoid(p[:, o + 3 * fox_w:] + fb_ref[...])

    u = p[:, :d_rnn]
    gate = p[:, d_rnn:o]

    tail = tail_ref[...]
    head_row = lax.broadcasted_iota(jnp.int32, (V7X_SUBLANES, d_rnn), 0)
    xc = cb_ref[...] + cw_ref[CONV_W - 1:CONV_W, :] * u
    for s in range(1, CONV_W):
        r = pltpu.roll(u, s, 0)
        head = jnp.where(head_row < s, pltpu.roll(tail, s, 0), r[:V7X_SUBLANES])
        shifted = jnp.concatenate([head, r[V7X_SUBLANES:]], axis=0)
        xc = xc + cw_ref[CONV_W - 1 - s:CONV_W - s, :] * shifted
    tail_ref[...] = u[tt - V7X_SUBLANES:, :]

    g = jnp.dot(xc.astype(BF16), gw_ref[...], preferred_element_type=F32) + gb_ref[...]
    r_gate = jax.nn.sigmoid(g[:, :d_rnn])
    i_gate = jax.nn.sigmoid(g[:, d_rnn:])
    log_a = -RG_C * r_gate * _softplus(-lam_ref[...])
    a = jnp.exp(log_a)
    one_minus_a2 = -jnp.tanh(log_a) * (a * a + 1.0)
    b = jnp.sqrt(one_minus_a2) * (i_gate * xc)

    chunk = min(SCAN_CHUNK, tt)
    chunk_row = lax.broadcasted_iota(jnp.int32, (chunk, d_rnn), 0)
    h_in = hc_ref[...]
    pieces = []
    for c in range(tt // chunk):
        big_a, big_b = _scan_linear(a[c * chunk:(c + 1) * chunk], b[c * chunk:(c + 1) * chunk], chunk_row)
        piece = big_a * h_in + big_b
        h_in = piece[chunk - 1:chunk, :]
        pieces.append(piece)
    hseq = jnp.concatenate(pieces, axis=0)
    hc_ref[...] = h_in
    y_ref[0] = (hseq * jax.nn.gelu(gate)).astype(BF16)

    lrow = lax.broadcasted_iota(jnp.int32, (tt, V7X_LANES), 0)
    c = _cumsum_rows(log_f, lrow) + cc_ref[...]
    cc_ref[...] = c[tt - 1:tt, :]
    c_ref[0] = c
    ct_ref[0] = c.T[:n_heads, :]


def _hymix(x3, g, w_pad, qk_gain, fb_pad, conv_w, conv_b, gate_w_dense, gate_b, lam, n_heads):
    bsz, s, d = x3.shape
    d_rnn = conv_w.shape[1]
    fox_w = n_heads * HEAD_DIM
    tt = min(TIME_TILE, s)
    blk = lambda b, i: (b, i, 0)
    return pl.pallas_call(
        functools.partial(_hymix_kernel, d_rnn=d_rnn, fox_w=fox_w, n_heads=n_heads),
        out_shape=(jax.ShapeDtypeStruct((bsz, s, fox_w), BF16),
                   jax.ShapeDtypeStruct((bsz, s, fox_w), BF16),
                   jax.ShapeDtypeStruct((bsz, s, fox_w), BF16),
                   jax.ShapeDtypeStruct((bsz, s, d_rnn), BF16),
                   jax.ShapeDtypeStruct((bsz, s, V7X_LANES), F32),
                   jax.ShapeDtypeStruct((bsz, n_heads, s), F32)),
        grid=(bsz, s // tt),
        in_specs=[pl.BlockSpec((1, tt, d), blk),
                  _resident((1, d)),
                  _resident(w_pad.shape),
                  _resident(qk_gain.shape),
                  _resident(fb_pad.shape),
                  _resident(conv_w.shape),
                  _resident((1, d_rnn)),
                  _resident(gate_w_dense.shape),
                  _resident((1, 2 * d_rnn)),
                  _resident((1, d_rnn))],
        out_specs=(pl.BlockSpec((1, tt, fox_w), blk),
                   pl.BlockSpec((1, tt, fox_w), blk),
                   pl.BlockSpec((1, tt, fox_w), blk),
                   pl.BlockSpec((1, tt, d_rnn), blk),
                   pl.BlockSpec((1, tt, V7X_LANES), blk),
                   pl.BlockSpec((1, n_heads, tt), lambda b, i: (b, 0, i))),
        scratch_shapes=[pltpu.VMEM((V7X_SUBLANES, d_rnn), F32),
                        pltpu.VMEM((1, d_rnn), F32),
                        pltpu.VMEM((1, V7X_LANES), F32)],
        compiler_params=_params(("parallel", "arbitrary")),
        name="hymix",
    )(x3, g.reshape(1, d), w_pad, qk_gain, fb_pad, conv_w, conv_b.reshape(1, d_rnn), gate_w_dense,
      gate_b.reshape(1, 2 * d_rnn), lam.reshape(1, d_rnn))


def _qk(qm, kj):
    return lax.dot_general(qm, kj, (((1,), (1,)), ((), ())), preferred_element_type=F32)


def _max_key_norms(k_ref, kmax_ref, chunk):
    s = k_ref.shape[1]
    first = lax.broadcasted_iota(jnp.int32, (chunk, V7X_LANES), 1) < HEAD_DIM
    first_row = lax.broadcasted_iota(jnp.int32, (1, V7X_LANES), 1) < HEAD_DIM
    m0 = jnp.zeros((1, 1), F32)
    m1 = jnp.zeros((1, 1), F32)
    for c in range(s // chunk):
        kk = k_ref[0, c * chunk:(c + 1) * chunk, :].astype(F32)
        sq = kk * kk
        n0 = jnp.sum(jnp.where(first, sq, 0.0), axis=-1, keepdims=True)
        n1 = jnp.sum(jnp.where(first, 0.0, sq), axis=-1, keepdims=True)
        m0 = jnp.maximum(m0, jnp.max(n0, axis=0, keepdims=True))
        m1 = jnp.maximum(m1, jnp.max(n1, axis=0, keepdims=True))
    kmax_ref[...] = jnp.sqrt(jnp.where(first_row, m0, m1))


def _logit_bound(qm, kmax_row, head_mask_row):
    qf = qm.astype(F32)
    qn = jnp.sqrt(jnp.sum(qf * qf, axis=-1, keepdims=True))
    km = jnp.max(jnp.where(head_mask_row, kmax_row, 0.0), axis=-1, keepdims=True)
    return qn * km * 1.01 + 1e-3


def _head_masks(rows):
    lane = lax.broadcasted_iota(jnp.int32, (rows, V7X_LANES), 1)
    lane_row = lax.broadcasted_iota(jnp.int32, (1, V7X_LANES), 1)
    full = [(lane >= hd * HEAD_DIM) & (lane < (hd + 1) * HEAD_DIM) for hd in range(HEADS_PER_TILE)]
    one = [(lane_row >= hd * HEAD_DIM) & (lane_row < (hd + 1) * HEAD_DIM)
           for hd in range(HEADS_PER_TILE)]
    return lane, full, one


def _sb_kernel(q_ref, k_ref, v_ref, o_ref, kmax_ref, *, tile):
    qi = pl.program_id(2)
    n_sub = q_ref.shape[1] // tile
    tq = tk = tile

    @pl.when(qi == 0)
    def _():
        _max_key_norms(k_ref, kmax_ref, min(1024, k_ref.shape[1]))

    _, head_masks, head_rows = _head_masks(tq)
    rows = lax.broadcasted_iota(jnp.int32, (tq, tk), 0)
    cols = lax.broadcasted_iota(jnp.int32, (tq, tk), 1)
    strictly_past = cols < rows
    tri = jnp.where(rows >= cols, 1.0, 0.0).astype(BF16)
    kmax_row = kmax_ref[...]

    def block(qneg, j, run, acc, diag):
        start = pl.multiple_of(j * tk, tk)
        kj = k_ref[0, pl.ds(start, tk), :]
        vj = v_ref[0, pl.ds(start, tk), :]
        nz = _qk(qneg, kj)
        if diag:
            nz = jnp.where(strictly_past, nz, -NEG_BIG)
        log_nb = _log_sigmoid(nz)
        incl = jnp.dot(log_nb.astype(BF16), tri, preferred_element_type=F32)
        w = jnp.exp(incl - nz if run is None else incl + run - nz)
        acc = acc + jnp.dot(w.astype(BF16), vj, preferred_element_type=F32)
        total = incl[:, 0:1]
        return (total if run is None else run + total), acc

    state = []
    for sub in range(n_sub):
        blk = qi * n_sub + sub
        q = q_ref[0, sub * tile:(sub + 1) * tile, :]
        left = jnp.maximum(blk - 1, 0)
        left_offset = jnp.where(blk >= 1, 0.0, NEG_BIG)
        for hd in range(HEADS_PER_TILE):
            qm = jnp.where(head_masks[hd], -q, jnp.zeros_like(q))
            bound = _logit_bound(qm, kmax_row, head_rows[hd])
            run, acc = block(qm, blk, None, jnp.zeros((tq, V7X_LANES), F32), True)
            run, acc = block(qm, left, run + left_offset, acc, False)
            state.append((blk, qm, bound, run, acc, jnp.max(run + bound) > EXP_ZERO_BELOW))

    results = []
    for blk, qm, bound, run, acc, go in state:

        def cond(carry):
            return jnp.logical_and(carry[0] >= 0, carry[1])

        def body(carry, qm=qm, bound=bound):
            j, _, run, acc = carry
            run, acc = block(qm, j, run, acc, False)
            return j - 1, jnp.max(run + bound) > EXP_ZERO_BELOW, run, acc

        results.append(lax.while_loop(cond, body, (blk - 2, go, run, acc))[3])

    for sub in range(n_sub):
        res = results[sub * HEADS_PER_TILE]
        for hd in range(1, HEADS_PER_TILE):
            res = jnp.where(head_masks[hd], results[sub * HEADS_PER_TILE + hd], res)
        o_ref[0, sub * tile:(sub + 1) * tile, :] = res.astype(BF16)


def _sb_attention(q, k, v):
    bsz, s, w = q.shape
    tile = min(SB_TILE, s)
    tq = min(SB_TILE * SB_SUBTILES, s)
    n_tiles = w // V7X_LANES
    return pl.pallas_call(
        functools.partial(_sb_kernel, tile=tile),
        out_shape=jax.ShapeDtypeStruct((bsz, s, w), BF16),
        grid=(bsz, n_tiles, s // tq),
        in_specs=[pl.BlockSpec((1, tq, V7X_LANES), lambda b, p, i: (b, i, p)),
                  pl.BlockSpec((1, s, V7X_LANES), lambda b, p, i: (b, 0, p)),
                  pl.BlockSpec((1, s, V7X_LANES), lambda b, p, i: (b, 0, p))],
        out_specs=pl.BlockSpec((1, tq, V7X_LANES), lambda b, p, i: (b, i, p)),
        scratch_shapes=[pltpu.VMEM((1, V7X_LANES), F32)],
        compiler_params=_params(("parallel", "parallel", "arbitrary")),
        name="sb_attn",
    )(q, k, v)


def _fox_kernel(q_ref, k_ref, v_ref, c_ref, ct_ref, o_ref, kmax_ref, *, tile):
    lane_tile = pl.program_id(1)
    qi = pl.program_id(2)
    n_sub = q_ref.shape[1] // tile
    tq = tk = tile

    @pl.when(qi == 0)
    def _():
        _max_key_norms(k_ref, kmax_ref, min(1024, k_ref.shape[1]))

    lane, head_masks, head_rows = _head_masks(tq)
    _, key_masks, _ = _head_masks(tk)
    rows = lax.broadcasted_iota(jnp.int32, (tq, tk), 0)
    cols = lax.broadcasted_iota(jnp.int32, (tq, tk), 1)
    causal = cols <= rows
    kmax_row = kmax_ref[...]

    def load(head, j):
        start = pl.multiple_of(j * tk, tk)
        kj = k_ref[0, pl.ds(start, tk), :]
        vj = v_ref[0, pl.ds(start, tk), :]
        ck = ct_ref[0, pl.ds(head, 1), pl.ds(start, tk)]
        return kj, vj, ck

    units = []
    for sub in range(n_sub):
        q = q_ref[0, sub * tile:(sub + 1) * tile, :]
        c_blk = c_ref[0, sub * tile:(sub + 1) * tile, :]
        for hd in range(HEADS_PER_TILE):
            head = lane_tile * HEADS_PER_TILE + hd
            qm = jnp.where(head_masks[hd], q, jnp.zeros_like(q))
            bound = _logit_bound(qm, kmax_row, head_rows[hd])
            cq = jnp.sum(jnp.where(lane == head, c_blk, 0.0), axis=-1, keepdims=True)
            units.append((qi * n_sub + sub, hd, head, qm, bound, cq))

    def merge(outs):
        tiles = []
        for sub in range(n_sub):
            res = outs[sub * HEADS_PER_TILE]
            for hd in range(1, HEADS_PER_TILE):
                res = jnp.where(head_masks[hd], outs[sub * HEADS_PER_TILE + hd], res)
            tiles.append(res)
        return jnp.concatenate(tiles, axis=0)

    def fixed_shift():
        def block(hd, head, qm, shift, cq_top, j, acc, diag):
            kj, vj, ck = load(head, j)
            s = _qk(qm, kj) + shift - ck
            if diag:
                s = jnp.where(causal, s, NEG_BIG)
            p = jnp.exp(s)
            v1 = jnp.where(key_masks[hd], vj, jnp.ones_like(vj))
            acc = acc + jnp.dot(p.astype(BF16), v1, preferred_element_type=F32)
            return acc, cq_top - jnp.max(ck) > EXP_ZERO_BELOW

        state = []
        for blk, hd, head, qm, bound, cq in units:
            shift = cq - bound
            cq_top = jnp.max(cq)
            acc, go = block(hd, head, qm, shift, cq_top, blk, jnp.zeros((tq, V7X_LANES), F32), True)
            for back in range(1, FOX_STRAIGHT_BLOCKS):
                offset = jnp.where(blk >= back, 0.0, NEG_BIG)
                acc, go = block(hd, head, qm, shift + offset, cq_top, jnp.maximum(blk - back, 0), acc, False)
            state.append((shift, cq_top, acc, go))

        outs = []
        for (blk, hd, head, qm, _, _), (shift, cq_top, acc, go) in zip(units, state):

            def cond(carry):
                return jnp.logical_and(carry[0] >= 0, carry[1])

            def body(carry, hd=hd, head=head, qm=qm, shift=shift, cq_top=cq_top):
                j, _, acc = carry
                acc, go = block(hd, head, qm, shift, cq_top, j, acc, False)
                return j - 1, go, acc

            _, _, acc = lax.while_loop(cond, body, (blk - FOX_STRAIGHT_BLOCKS, go, acc))
            outs.append(acc / pltpu.roll(acc, HEAD_DIM, 1))
        return merge(outs)

    def running_shift():
        outs = []
        for blk, _, head, qm, bound, cq in units:

            def block(j, m, l, acc, diag, head=head, qm=qm, bound=bound, cq=cq):
                kj, vj, ck = load(head, j)
                s = _qk(qm, kj) + cq - ck
                if diag:
                    s = jnp.where(causal, s, NEG_BIG)
                m_new = jnp.maximum(m, jnp.max(s, axis=-1, keepdims=True))
                alpha = jnp.exp(m - m_new)
                p = jnp.exp(s - m_new)
                l = alpha * l + jnp.sum(p, axis=-1, keepdims=True)
                acc = alpha * acc + jnp.dot(p.astype(BF16), vj, preferred_element_type=F32)
                reach = cq - jnp.max(ck, axis=-1, keepdims=True)
                return m_new, l, acc, jnp.max(bound + reach - m_new) > EXP_ZERO_BELOW

            m, l, acc, go = block(blk, jnp.full((tq, 1), NEG_BIG, F32), jnp.zeros((tq, 1), F32),
                                  jnp.zeros((tq, V7X_LANES), F32), True)

            def cond(carry):
                return jnp.logical_and(carry[0] >= 0, carry[1])

            def body(carry, block=block):
                j, _, m, l, acc = carry
                m, l, acc, go = block(j, m, l, acc, False)
                return j - 1, go, m, l, acc

            _, _, _, l, acc = lax.while_loop(cond, body, (blk - 1, go, m, l, acc))
            outs.append(acc / l)
        return merge(outs)

    largest_bound = functools.reduce(jnp.maximum, [jnp.max(u[4]) for u in units])
    res = lax.cond(largest_bound <= FIXED_SHIFT_MAX_BOUND, fixed_shift, running_shift)
    o_ref[0] = res.astype(BF16)


def _fox_attention(q, k, v, c, ct):
    bsz, s, w = q.shape
    n_heads = ct.shape[1]
    tile = min(FOX_TILE, s)
    tq = min(FOX_TILE * FOX_SUBTILES, s)
    n_tiles = w // V7X_LANES
    return pl.pallas_call(
        functools.partial(_fox_kernel, tile=tile),
        out_shape=jax.ShapeDtypeStruct((bsz, s, w), BF16),
        grid=(bsz, n_tiles, s // tq),
        in_specs=[pl.BlockSpec((1, tq, V7X_LANES), lambda b, p, i: (b, i, p)),
                  pl.BlockSpec((1, s, V7X_LANES), lambda b, p, i: (b, 0, p)),
                  pl.BlockSpec((1, s, V7X_LANES), lambda b, p, i: (b, 0, p)),
                  pl.BlockSpec((1, tq, V7X_LANES), lambda b, p, i: (b, i, 0)),
                  pl.BlockSpec((1, n_heads, s), lambda b, p, i: (b, 0, 0))],
        out_specs=pl.BlockSpec((1, tq, V7X_LANES), lambda b, p, i: (b, i, p)),
        scratch_shapes=[pltpu.VMEM((1, V7X_LANES), F32)],
        compiler_params=_params(("parallel", "parallel", "arbitrary")),
        name="fox_attn",
    )(q, k, v, c, ct)


def _sbin_kernel(x_ref, g_ref, w_ref, q_ref, k_ref, v_ref, *, width):
    x = x_ref[...]
    h = _rmsnorm_rows(x, g_ref[...]).astype(BF16)
    p = jnp.dot(h, w_ref[...], preferred_element_type=F32)
    q_ref[...] = (p[:, :width] * (HEAD_DIM ** -0.5)).astype(BF16)
    k_ref[...] = p[:, width:2 * width].astype(BF16)
    v_ref[...] = p[:, 2 * width:].astype(BF16)


def _sbin(x2, g, w_qkv):
    t, d = x2.shape
    width = w_qkv.shape[1] // 3
    tm = min(TOKEN_TILE, t)
    row = lambda i: (i, 0)
    return pl.pallas_call(
        functools.partial(_sbin_kernel, width=width),
        out_shape=tuple(jax.ShapeDtypeStruct((t, width), BF16) for _ in range(3)),
        grid=(t // tm,),
        in_specs=[pl.BlockSpec((tm, d), row), _resident((1, d)), _resident(w_qkv.shape)],
        out_specs=tuple(pl.BlockSpec((tm, width), row) for _ in range(3)),
        compiler_params=_params(("parallel",)),
        name="sbin",
    )(x2, g.reshape(1, d), w_qkv)


def _block_diag_gates(gate_w):
    n_gates, n_blocks, r, _ = gate_w.shape
    eye = jnp.eye(n_blocks, dtype=gate_w.dtype)
    dense = jnp.einsum('ghij,hk->ghikj', gate_w, eye).reshape(n_gates, n_blocks * r, n_blocks * r)
    return jnp.concatenate([dense[g] for g in range(n_gates)], axis=1)


def kernel(x, ffn_norm, ffn_w_in, ffn_w_out, mix_norm, hy_w_in, rg_conv_w, rg_conv_b, rg_gate_w,
           rg_gate_b, rg_lambda, fox_fgate_b, fox_qk_norm, hy_w_out, sb_w_qkv, sb_w_out):
    bsz, s, d = x.shape
    depth = ffn_norm.shape[0]
    d_rnn = rg_conv_w.shape[2]
    n_fox = fox_fgate_b.shape[1]
    fox_w = n_fox * HEAD_DIM
    t = bsz * s

    x2 = x.reshape(t, d)
    w_in_all = ffn_w_in.astype(BF16)
    w_out_all = ffn_w_out.astype(BF16)
    for layer in range(depth):
        x2 = _ffn(x2, ffn_norm[layer, 0], w_in_all, w_out_all, (layer, 0))
        if layer % 2 == 0:
            e = layer // 2
            pad = V7X_LANES - n_fox
            w_pad = jnp.pad(hy_w_in[e], ((0, 0), (0, pad))).astype(BF16)
            fb_pad = jnp.pad(fox_fgate_b[e], (0, pad)).reshape(1, V7X_LANES)
            qk_gain = jnp.tile(fox_qk_norm[e], (1, HEADS_PER_TILE))
            q, k, v, y_rnn, c, ct = _hymix(x2.reshape(bsz, s, d), mix_norm[layer], w_pad, qk_gain, fb_pad,
                                           rg_conv_w[e], rg_conv_b[e],
                                           _block_diag_gates(rg_gate_w[e]).astype(BF16),
                                           rg_gate_b[e].reshape(-1), rg_lambda[e], n_fox)
            y_fox = _fox_attention(q, k, v, c, ct)
            w_out = hy_w_out[e].astype(BF16)
            mixer_out = [(y_rnn.reshape(t, d_rnn), w_out[:d_rnn]),
                         (y_fox.reshape(t, fox_w), w_out[d_rnn:])]
        else:
            o = layer // 2
            q, k, v = _sbin(x2, mix_norm[layer], sb_w_qkv[o].astype(BF16))
            width = q.shape[1]
            y = _sb_attention(q.reshape(bsz, s, width), k.reshape(bsz, s, width),
                              v.reshape(bsz, s, width))
            mixer_out = [(y.reshape(t, width), sb_w_out[o].astype(BF16))]
        x2 = _ffn(x2, ffn_norm[layer, 1], w_in_all, w_out_all, (layer, 1), mixer_out)
    return x2.reshape(bsz, s, d)
```

```python
import functools

import jax
import jax.numpy as jnp
from jax import lax
from jax.experimental import pallas as pl
from jax.experimental.pallas import tpu as pltpu

F32 = jnp.float32
BF16 = jnp.bfloat16

HEAD_DIM = 64
RMS_EPS = 1e-6
RG_C = 8.0
CONV_W = 4

V7X_LANES = 128
V7X_SUBLANES = 8
V7X_VMEM_LIMIT_BYTES = 56 * 1024 * 1024
HEADS_PER_TILE = V7X_LANES // HEAD_DIM

EXP_ZERO_BELOW = -104.0
NEG_BIG = -1e30
LOG2E = 1.4426950408889634
FIXED_SHIFT_MAX_BOUND = 30.0

TOKEN_TILE = 512
FFN_TILE = 1024
TIME_TILE = 512
SCAN_CHUNK = 8
SB_TILE = 256
SB_SUBTILES = 2
FOX_TILE = 512
FOX_SUBTILES = 2
FOX_STRAIGHT_BLOCKS = 3


def _params(semantics):
    return pltpu.CompilerParams(dimension_semantics=semantics,
                                vmem_limit_bytes=V7X_VMEM_LIMIT_BYTES)


def _resident(shape):
    nd = len(shape)
    return pl.BlockSpec(shape, lambda *_: (0,) * nd, pipeline_mode=pl.Buffered(1))


def _rmsnorm_rows(x, g):
    ms = jnp.mean(x * x, axis=-1, keepdims=True)
    return x * lax.rsqrt(ms + RMS_EPS) * g


def _softplus(x):
    return jnp.maximum(x, 0.0) + jnp.log(1.0 + jnp.exp(-jnp.abs(x)))


def _sigmoid(x):
    return 1.0 / (1.0 + jnp.exp(-x))


def _log_sigmoid(x):
    return jnp.minimum(x, 0.0) - jnp.log(1.0 + jnp.exp2(jnp.abs(x) * (-LOG2E)))


def _ffn_kernel(*refs, d_ff, n_parts, sub_rows):
    x_ref, g_ref, win_ref, wout_ref = refs[:4]
    o_ref = refs[-1]
    for r in range(x_ref.shape[0] // sub_rows):
        rows = slice(r * sub_rows, (r + 1) * sub_rows)
        x = x_ref[rows, :]
        for i in range(n_parts):
            x = x + jnp.dot(refs[4 + 2 * i][rows, :], refs[5 + 2 * i][...], preferred_element_type=F32)
        h = _rmsnorm_rows(x, g_ref[...]).astype(BF16)
        ab = jnp.dot(h, win_ref[...], preferred_element_type=F32)
        a = ab[:, :d_ff]
        b = ab[:, d_ff:]
        t = (a * _sigmoid(a) * b).astype(BF16)
        y = jnp.dot(t, wout_ref[...], preferred_element_type=F32)
        o_ref[rows, :] = x + 0.5 * y


def _ffn(x2, g, w_in_all, w_out_all, which, parts=()):
    t, d = x2.shape
    d_ff = w_out_all.shape[2]
    tm = min(FFN_TILE, t)
    row = lambda i: (i, 0)
    pick = lambda i: (which[0], which[1], 0, 0)
    in_specs = [pl.BlockSpec((tm, d), row), _resident((1, d)),
                pl.BlockSpec((None, None) + w_in_all.shape[2:], pick, pipeline_mode=pl.Buffered(1)),
                pl.BlockSpec((None, None) + w_out_all.shape[2:], pick, pipeline_mode=pl.Buffered(1))]
    args = [x2, g.reshape(1, d), w_in_all, w_out_all]
    for y, w in parts:
        in_specs += [pl.BlockSpec((tm, y.shape[1]), row), _resident(w.shape)]
        args += [y, w]
    return pl.pallas_call(
        functools.partial(_ffn_kernel, d_ff=d_ff, n_parts=len(parts), sub_rows=min(TOKEN_TILE, tm)),
        out_shape=jax.ShapeDtypeStruct((t, d), F32),
        grid=(t // tm,),
        in_specs=in_specs,
        out_specs=pl.BlockSpec((tm, d), row),
        compiler_params=_params(("parallel",)),
        name="ffn",
    )(*args)


def _head_rmsnorm(t, gain_row, first_head):
    outs = []
    for j in range(t.shape[1] // V7X_LANES):
        tj = t[:, j * V7X_LANES:(j + 1) * V7X_LANES]
        sq = tj * tj
        s0 = jnp.sum(jnp.where(first_head, sq, 0.0), axis=-1, keepdims=True)
        s1 = jnp.sum(jnp.where(first_head, 0.0, sq), axis=-1, keepdims=True)
        ms = jnp.where(first_head, s0, s1) * (1.0 / HEAD_DIM)
        outs.append(tj * lax.rsqrt(ms + RMS_EPS) * gain_row)
    return jnp.concatenate(outs, axis=-1)


def _scan_linear(a, b, row):
    n = a.shape[0]
    s = 1
    while s < n:
        keep = row >= s
        b = jnp.where(keep, a * pltpu.roll(b, s, 0) + b, b)
        a = jnp.where(keep, a * pltpu.roll(a, s, 0), a)
        s *= 2
    return a, b


def _cumsum_rows(x, row):
    n = x.shape[0]
    s = 1
    while s < n:
        x = jnp.where(row >= s, x + pltpu.roll(x, s, 0), x)
        s *= 2
    return x


def _hymix_kernel(x_ref, g_ref, w_ref, qkg_ref, fb_ref, cw_ref, cb_ref, gw_ref, gb_ref, lam_ref,
                  q_ref, k_ref, v_ref, y_ref, c_ref, ct_ref, tail_ref, hc_ref, cc_ref,
                  *, d_rnn, fox_w, n_heads):
    @pl.when(pl.program_id(1) == 0)
    def _():
        tail_ref[...] = jnp.zeros_like(tail_ref)
        hc_ref[...] = jnp.zeros_like(hc_ref)
        cc_ref[...] = jnp.zeros_like(cc_ref)

    x = x_ref[0]
    tt = x.shape[0]
    h = _rmsnorm_rows(x, g_ref[...]).astype(BF16)
    p = jnp.dot(h, w_ref[...], preferred_element_type=F32)
    o = 2 * d_rnn
    first_head = lax.broadcasted_iota(jnp.int32, (tt, V7X_LANES), 1) < HEAD_DIM
    scale = HEAD_DIM ** -0.5
    q = _head_rmsnorm(p[:, o:o + fox_w], qkg_ref[0:1, :], first_head)
    q_ref[0] = (q * scale).astype(BF16)
    k = _head_rmsnorm(p[:, o + fox_w:o + 2 * fox_w], qkg_ref[1:2, :], first_head)
    k_ref[0] = k.astype(BF16)
    v_ref[0] = p[:, o + 2 * fox_w:o + 3 * fox_w].astype(BF16)
    log_f = _log_sigmoid(p[:, o + 3 * fox_w:] + fb_ref[...])

    u = p[:, :d_rnn]
    gate = p[:, d_rnn:o]

    tail = tail_ref[...]
    head_row = lax.broadcasted_iota(jnp.int32, (V7X_SUBLANES, d_rnn), 0)
    xc = cb_ref[...] + cw_ref[CONV_W - 1:CONV_W, :] * u
    for s in range(1, CONV_W):
        r = pltpu.roll(u, s, 0)
        head = jnp.where(head_row < s, pltpu.roll(tail, s, 0), r[:V7X_SUBLANES])
        shifted = jnp.concatenate([head, r[V7X_SUBLANES:]], axis=0)
        xc = xc + cw_ref[CONV_W - 1 - s:CONV_W - s, :] * shifted
    tail_ref[...] = u[tt - V7X_SUBLANES:, :]

    g = jnp.dot(xc.astype(BF16), gw_ref[...], preferred_element_type=F32) + gb_ref[...]
    r_gate = jax.nn.sigmoid(g[:, :d_rnn])
    i_gate = jax.nn.sigmoid(g[:, d_rnn:])
    log_a = -RG_C * r_gate * _softplus(-lam_ref[...])
    a = jnp.exp(log_a)
    one_minus_a2 = -jnp.tanh(log_a) * (a * a + 1.0)
    b = jnp.sqrt(one_minus_a2) * (i_gate * xc)

    chunk = min(SCAN_CHUNK, tt)
    chunk_row = lax.broadcasted_iota(jnp.int32, (chunk, d_rnn), 0)
    h_in = hc_ref[...]
    pieces = []
    for c in range(tt // chunk):
        big_a, big_b = _scan_linear(a[c * chunk:(c + 1) * chunk], b[c * chunk:(c + 1) * chunk], chunk_row)
        piece = big_a * h_in + big_b
        h_in = piece[chunk - 1:chunk, :]
        pieces.append(piece)
    hseq = jnp.concatenate(pieces, axis=0)
    hc_ref[...] = h_in
    y_ref[0] = (hseq * jax.nn.gelu(gate)).astype(BF16)

    lrow = lax.broadcasted_iota(jnp.int32, (tt, V7X_LANES), 0)
    c = _cumsum_rows(log_f, lrow) + cc_ref[...]
    cc_ref[...] = c[tt - 1:tt, :]
    c_ref[0] = c
    ct_ref[0] = c.T[:n_heads, :]


def _hymix(x3, g, w_pad, qk_gain, fb_pad, conv_w, conv_b, gate_w_dense, gate_b, lam, n_heads):
    bsz, s, d = x3.shape
    d_rnn = conv_w.shape[1]
    fox_w = n_heads * HEAD_DIM
    tt = min(TIME_TILE, s)
    blk = lambda b, i: (b, i, 0)
    return pl.pallas_call(
        functools.partial(_hymix_kernel, d_rnn=d_rnn, fox_w=fox_w, n_heads=n_heads),
        out_shape=(jax.ShapeDtypeStruct((bsz, s, fox_w), BF16),
                   jax.ShapeDtypeStruct((bsz, s, fox_w), BF16),
                   jax.ShapeDtypeStruct((bsz, s, fox_w), BF16),
                   jax.ShapeDtypeStruct((bsz, s, d_rnn), BF16),
                   jax.ShapeDtypeStruct((bsz, s, V7X_LANES), F32),
                   jax.ShapeDtypeStruct((bsz, n_heads, s), F32)),
        grid=(bsz, s // tt),
        in_specs=[pl.BlockSpec((1, tt, d), blk),
                  _resident((1, d)),
                  _resident(w_pad.shape),
                  _resident(qk_gain.shape),
                  _resident(fb_pad.shape),
                  _resident(conv_w.shape),
                  _resident((1, d_rnn)),
                  _resident(gate_w_dense.shape),
                  _resident((1, 2 * d_rnn)),
                  _resident((1, d_rnn))],
        out_specs=(pl.BlockSpec((1, tt, fox_w), blk),
                   pl.BlockSpec((1, tt, fox_w), blk),
                   pl.BlockSpec((1, tt, fox_w), blk),
                   pl.BlockSpec((1, tt, d_rnn), blk),
                   pl.BlockSpec((1, tt, V7X_LANES), blk),
                   pl.BlockSpec((1, n_heads, tt), lambda b, i: (b, 0, i))),
        scratch_shapes=[pltpu.VMEM((V7X_SUBLANES, d_rnn), F32),
                        pltpu.VMEM((1, d_rnn), F32),
                        pltpu.VMEM((1, V7X_LANES), F32)],
        compiler_params=_params(("parallel", "arbitrary")),
        name="hymix",
    )(x3, g.reshape(1, d), w_pad, qk_gain, fb_pad, conv_w, conv_b.reshape(1, d_rnn), gate_w_dense,
      gate_b.reshape(1, 2 * d_rnn), lam.reshape(1, d_rnn))


def _qk(qm, kj):
    return lax.dot_general(qm, kj, (((1,), (1,)), ((), ())), preferred_element_type=F32)


def _max_key_norms(k_ref, kmax_ref, chunk):
    s = k_ref.shape[1]
    gi = lax.broadcasted_iota(jnp.int32, (V7X_LANES, V7X_LANES), 0) < HEAD_DIM
    gj = lax.broadcasted_iota(jnp.int32, (V7X_LANES, V7X_LANES), 1) < HEAD_DIM
    same_head = jnp.where(gi == gj, 1.0, 0.0).astype(BF16)
    m = jnp.zeros((1, V7X_LANES), F32)
    for c in range(s // chunk):
        kk = k_ref[0, c * chunk:(c + 1) * chunk, :].astype(F32)
        sq = jnp.dot((kk * kk).astype(BF16), same_head, preferred_element_type=F32)
        m = jnp.maximum(m, jnp.max(sq, axis=0, keepdims=True))
    kmax_ref[...] = jnp.sqrt(m)


def _logit_bound(qm, kmax_row, head_mask_row):
    qf = qm.astype(F32)
    qn = jnp.sqrt(jnp.sum(qf * qf, axis=-1, keepdims=True))
    km = jnp.max(jnp.where(head_mask_row, kmax_row, 0.0), axis=-1, keepdims=True)
    return qn * km * 1.01 + 1e-3


def _head_masks(rows):
    lane = lax.broadcasted_iota(jnp.int32, (rows, V7X_LANES), 1)
    lane_row = lax.broadcasted_iota(jnp.int32, (1, V7X_LANES), 1)
    full = [(lane >= hd * HEAD_DIM) & (lane < (hd + 1) * HEAD_DIM) for hd in range(HEADS_PER_TILE)]
    one = [(lane_row >= hd * HEAD_DIM) & (lane_row < (hd + 1) * HEAD_DIM)
           for hd in range(HEADS_PER_TILE)]
    return lane, full, one


def _sb_kernel(q_ref, k_ref, v_ref, o_ref, kmax_ref, *, tile):
    qi = pl.program_id(2)
    n_sub = q_ref.shape[1] // tile
    tq = tk = tile

    @pl.when(qi == 0)
    def _():
        _max_key_norms(k_ref, kmax_ref, min(1024, k_ref.shape[1]))

    _, head_masks, head_rows = _head_masks(tq)
    rows = lax.broadcasted_iota(jnp.int32, (tq, tk), 0)
    cols = lax.broadcasted_iota(jnp.int32, (tq, tk), 1)
    strictly_past = cols < rows
    tri = jnp.where(rows >= cols, 1.0, 0.0).astype(BF16)
    kmax_row = kmax_ref[...]

    def block(qneg, j, run, acc, diag):
        start = pl.multiple_of(j * tk, tk)
        kj = k_ref[0, pl.ds(start, tk), :]
        vj = v_ref[0, pl.ds(start, tk), :]
        nz = _qk(qneg, kj)
        if diag:
            nz = jnp.where(strictly_past, nz, -NEG_BIG)
        log_nb = _log_sigmoid(nz)
        incl = jnp.dot(log_nb.astype(BF16), tri, preferred_element_type=F32)
        w = jnp.exp(incl - nz if run is None else incl + run - nz)
        acc = acc + jnp.dot(w.astype(BF16), vj, preferred_element_type=F32)
        total = incl[:, 0:1]
        return (total if run is None else run + total), acc

    state = []
    for sub in range(n_sub):
        blk = qi * n_sub + sub
        q = q_ref[0, sub * tile:(sub + 1) * tile, :]
        left = jnp.maximum(blk - 1, 0)
        left_offset = jnp.where(blk >= 1, 0.0, NEG_BIG)
        for hd in range(HEADS_PER_TILE):
            qm = jnp.where(head_masks[hd], -q, jnp.zeros_like(q))
            bound = _logit_bound(qm, kmax_row, head_rows[hd])
            run, acc = block(qm, blk, None, jnp.zeros((tq, V7X_LANES), F32), True)
            run, acc = block(qm, left, run + left_offset, acc, False)
            state.append((blk, qm, bound, run, acc, jnp.max(run + bound) > EXP_ZERO_BELOW))

    results = []
    for blk, qm, bound, run, acc, go in state:

        def cond(carry):
            return jnp.logical_and(carry[0] >= 0, carry[1])

        def body(carry, qm=qm, bound=bound):
            j, _, run, acc = carry
            run, acc = block(qm, j, run, acc, False)
            return j - 1, jnp.max(run + bound) > EXP_ZERO_BELOW, run, acc

        results.append(lax.while_loop(cond, body, (blk - 2, go, run, acc))[3])

    for sub in range(n_sub):
        res = results[sub * HEADS_PER_TILE]
        for hd in range(1, HEADS_PER_TILE):
            res = jnp.where(head_masks[hd], results[sub * HEADS_PER_TILE + hd], res)
        o_ref[0, sub * tile:(sub + 1) * tile, :] = res.astype(BF16)


def _sb_attention(q, k, v):
    bsz, s, w = q.shape
    tile = min(SB_TILE, s)
    tq = min(SB_TILE * SB_SUBTILES, s)
    n_tiles = w // V7X_LANES
    return pl.pallas_call(
        functools.partial(_sb_kernel, tile=tile),
        out_shape=jax.ShapeDtypeStruct((bsz, s, w), BF16),
        grid=(bsz, n_tiles, s // tq),
        in_specs=[pl.BlockSpec((1, tq, V7X_LANES), lambda b, p, i: (b, i, p)),
                  pl.BlockSpec((1, s, V7X_LANES), lambda b, p, i: (b, 0, p)),
                  pl.BlockSpec((1, s, V7X_LANES), lambda b, p, i: (b, 0, p))],
        out_specs=pl.BlockSpec((1, tq, V7X_LANES), lambda b, p, i: (b, i, p)),
        scratch_shapes=[pltpu.VMEM((1, V7X_LANES), F32)],
        compiler_params=_params(("parallel", "parallel", "arbitrary")),
        name="sb_attn",
    )(q, k, v)


def _fox_kernel(q_ref, k_ref, v_ref, c_ref, ct_ref, o_ref, kmax_ref, *, tile):
    lane_tile = pl.program_id(1)
    qi = pl.program_id(2)
    n_sub = q_ref.shape[1] // tile
    tq = tk = tile

    @pl.when(qi == 0)
    def _():
        _max_key_norms(k_ref, kmax_ref, min(1024, k_ref.shape[1]))

    lane, head_masks, head_rows = _head_masks(tq)
    _, key_masks, _ = _head_masks(tk)
    rows = lax.broadcasted_iota(jnp.int32, (tq, tk), 0)
    cols = lax.broadcasted_iota(jnp.int32, (tq, tk), 1)
    causal = cols <= rows
    kmax_row = kmax_ref[...]

    def load(head, j):
        start = pl.multiple_of(j * tk, tk)
        kj = k_ref[0, pl.ds(start, tk), :]
        vj = v_ref[0, pl.ds(start, tk), :]
        ck = ct_ref[0, pl.ds(head, 1), pl.ds(start, tk)]
        return kj, vj, ck

    units = []
    for sub in range(n_sub):
        q = q_ref[0, sub * tile:(sub + 1) * tile, :]
        c_blk = c_ref[0, sub * tile:(sub + 1) * tile, :]
        for hd in range(HEADS_PER_TILE):
            head = lane_tile * HEADS_PER_TILE + hd
            qm = jnp.where(head_masks[hd], q, jnp.zeros_like(q))
            bound = _logit_bound(qm, kmax_row, head_rows[hd])
            cq = jnp.sum(jnp.where(lane == head, c_blk, 0.0), axis=-1, keepdims=True)
            units.append((qi * n_sub + sub, hd, head, qm, bound, cq))

    def merge(outs):
        tiles = []
        for sub in range(n_sub):
            res = outs[sub * HEADS_PER_TILE]
            for hd in range(1, HEADS_PER_TILE):
                res = jnp.where(head_masks[hd], outs[sub * HEADS_PER_TILE + hd], res)
            tiles.append(res)
        return jnp.concatenate(tiles, axis=0)

    def fixed_shift():
        def block(hd, head, qm, shift, cq_top, j, acc, diag):
            kj, vj, ck = load(head, j)
            s = _qk(qm, kj) + shift - ck
            if diag:
                s = jnp.where(causal, s, NEG_BIG)
            p = jnp.exp(s)
            v1 = jnp.where(key_masks[hd], vj, jnp.ones_like(vj))
            acc = acc + jnp.dot(p.astype(BF16), v1, preferred_element_type=F32)
            return acc, cq_top - jnp.max(ck) > EXP_ZERO_BELOW

        state = []
        for blk, hd, head, qm, bound, cq in units:
            shift = cq - bound
            cq_top = jnp.max(cq)
            acc, go = block(hd, head, qm, shift, cq_top, blk, jnp.zeros((tq, V7X_LANES), F32), True)
            for back in range(1, FOX_STRAIGHT_BLOCKS):
                offset = jnp.where(blk >= back, 0.0, NEG_BIG)
                acc, go = block(hd, head, qm, shift + offset, cq_top, jnp.maximum(blk - back, 0), acc, False)
            state.append((shift, cq_top, acc, go))

        outs = []
        for (blk, hd, head, qm, _, _), (shift, cq_top, acc, go) in zip(units, state):

            def cond(carry):
                return jnp.logical_and(carry[0] >= 0, carry[1])

            def body(carry, hd=hd, head=head, qm=qm, shift=shift, cq_top=cq_top):
                j, _, acc = carry
                acc, go = block(hd, head, qm, shift, cq_top, j, acc, False)
                return j - 1, go, acc

            _, _, acc = lax.while_loop(cond, body, (blk - FOX_STRAIGHT_BLOCKS, go, acc))
            outs.append(acc / pltpu.roll(acc, HEAD_DIM, 1))
        return merge(outs)

    def running_shift():
        outs = []
        for blk, _, head, qm, bound, cq in units:

            def block(j, m, l, acc, diag, head=head, qm=qm, bound=bound, cq=cq):
                kj, vj, ck = load(head, j)
                s = _qk(qm, kj) + cq - ck
                if diag:
                    s = jnp.where(causal, s, NEG_BIG)
                m_new = jnp.maximum(m, jnp.max(s, axis=-1, keepdims=True))
                alpha = jnp.exp(m - m_new)
                p = jnp.exp(s - m_new)
                l = alpha * l + jnp.sum(p, axis=-1, keepdims=True)
                acc = alpha * acc + jnp.dot(p.astype(BF16), vj, preferred_element_type=F32)
                reach = cq - jnp.max(ck, axis=-1, keepdims=True)
                return m_new, l, acc, jnp.max(bound + reach - m_new) > EXP_ZERO_BELOW

            m, l, acc, go = block(blk, jnp.full((tq, 1), NEG_BIG, F32), jnp.zeros((tq, 1), F32),
                                  jnp.zeros((tq, V7X_LANES), F32), True)

            def cond(carry):
                return jnp.logical_and(carry[0] >= 0, carry[1])

            def body(carry, block=block):
                j, _, m, l, acc = carry
                m, l, acc, go = block(j, m, l, acc, False)
                return j - 1, go, m, l, acc

            _, _, _, l, acc = lax.while_loop(cond, body, (blk - 1, go, m, l, acc))
            outs.append(acc / l)
        return merge(outs)

    largest_bound = functools.reduce(jnp.maximum, [jnp.max(u[4]) for u in units])
    res = lax.cond(largest_bound <= FIXED_SHIFT_MAX_BOUND, fixed_shift, running_shift)
    o_ref[0] = res.astype(BF16)


def _fox_attention(q, k, v, c, ct):
    bsz, s, w = q.shape
    n_heads = ct.shape[1]
    tile = min(FOX_TILE, s)
    tq = min(FOX_TILE * FOX_SUBTILES, s)
    n_tiles = w // V7X_LANES
    return pl.pallas_call(
        functools.partial(_fox_kernel, tile=tile),
        out_shape=jax.ShapeDtypeStruct((bsz, s, w), BF16),
        grid=(bsz, n_tiles, s // tq),
        in_specs=[pl.BlockSpec((1, tq, V7X_LANES), lambda b, p, i: (b, i, p)),
                  pl.BlockSpec((1, s, V7X_LANES), lambda b, p, i: (b, 0, p)),
                  pl.BlockSpec((1, s, V7X_LANES), lambda b, p, i: (b, 0, p)),
                  pl.BlockSpec((1, tq, V7X_LANES), lambda b, p, i: (b, i, 0)),
                  pl.BlockSpec((1, n_heads, s), lambda b, p, i: (b, 0, 0))],
        out_specs=pl.BlockSpec((1, tq, V7X_LANES), lambda b, p, i: (b, i, p)),
        scratch_shapes=[pltpu.VMEM((1, V7X_LANES), F32)],
        compiler_params=_params(("parallel", "parallel", "arbitrary")),
        name="fox_attn",
    )(q, k, v, c, ct)


def _sbin_kernel(x_ref, g_ref, w_ref, q_ref, k_ref, v_ref, *, width):
    x = x_ref[...]
    h = _rmsnorm_rows(x, g_ref[...]).astype(BF16)
    p = jnp.dot(h, w_ref[...], preferred_element_type=F32)
    q_ref[...] = (p[:, :width] * (HEAD_DIM ** -0.5)).astype(BF16)
    k_ref[...] = p[:, width:2 * width].astype(BF16)
    v_ref[...] = p[:, 2 * width:].astype(BF16)


def _sbin(x2, g, w_qkv):
    t, d = x2.shape
    width = w_qkv.shape[1] // 3
    tm = min(TOKEN_TILE, t)
    row = lambda i: (i, 0)
    return pl.pallas_call(
        functools.partial(_sbin_kernel, width=width),
        out_shape=tuple(jax.ShapeDtypeStruct((t, width), BF16) for _ in range(3)),
        grid=(t // tm,),
        in_specs=[pl.BlockSpec((tm, d), row), _resident((1, d)), _resident(w_qkv.shape)],
        out_specs=tuple(pl.BlockSpec((tm, width), row) for _ in range(3)),
        compiler_params=_params(("parallel",)),
        name="sbin",
    )(x2, g.reshape(1, d), w_qkv)


def _block_diag_gates(gate_w):
    n_gates, n_blocks, r, _ = gate_w.shape
    eye = jnp.eye(n_blocks, dtype=gate_w.dtype)
    dense = jnp.einsum('ghij,hk->ghikj', gate_w, eye).reshape(n_gates, n_blocks * r, n_blocks * r)
    return jnp.concatenate([dense[g] for g in range(n_gates)], axis=1)


def kernel(x, ffn_norm, ffn_w_in, ffn_w_out, mix_norm, hy_w_in, rg_conv_w, rg_conv_b, rg_gate_w,
           rg_gate_b, rg_lambda, fox_fgate_b, fox_qk_norm, hy_w_out, sb_w_qkv, sb_w_out):
    bsz, s, d = x.shape
    depth = ffn_norm.shape[0]
    d_rnn = rg_conv_w.shape[2]
    n_fox = fox_fgate_b.shape[1]
    fox_w = n_fox * HEAD_DIM
    t = bsz * s

    x2 = x.reshape(t, d)
    w_in_all = ffn_w_in.astype(BF16)
    w_out_all = ffn_w_out.astype(BF16)
    for layer in range(depth):
        x2 = _ffn(x2, ffn_norm[layer, 0], w_in_all, w_out_all, (layer, 0))
        if layer % 2 == 0:
            e = layer // 2
            pad = V7X_LANES - n_fox
            w_pad = jnp.pad(hy_w_in[e], ((0, 0), (0, pad))).astype(BF16)
            fb_pad = jnp.pad(fox_fgate_b[e], (0, pad)).reshape(1, V7X_LANES)
            qk_gain = jnp.tile(fox_qk_norm[e], (1, HEADS_PER_TILE))
            q, k, v, y_rnn, c, ct = _hymix(x2.reshape(bsz, s, d), mix_norm[layer], w_pad, qk_gain, fb_pad,
                                           rg_conv_w[e], rg_conv_b[e],
                                           _block_diag_gates(rg_gate_w[e]).astype(BF16),
                                           rg_gate_b[e].reshape(-1), rg_lambda[e], n_fox)
            y_fox = _fox_attention(q, k, v, c, ct)
            w_out = hy_w_out[e].astype(BF16)
            mixer_out = [(y_rnn.reshape(t, d_rnn), w_out[:d_rnn]),
                         (y_fox.reshape(t, fox_w), w_out[d_rnn:])]
        else:
            o = layer // 2
            q, k, v = _sbin(x2, mix_norm[layer], sb_w_qkv[o].astype(BF16))
            width = q.shape[1]
            y = _sb_attention(q.reshape(bsz, s, width), k.reshape(bsz, s, width),
                              v.reshape(bsz, s, width))
            mixer_out = [(y.reshape(t, width), sb_w_out[o].astype(BF16))]
        x2 = _ffn(x2, ffn_norm[layer, 1], w_in_all, w_out_all, (layer, 1), mixer_out)
    return x2.reshape(bsz, s, d)
```

```python
import functools

import jax
import jax.numpy as jnp
from jax import lax
from jax.experimental import pallas as pl
from jax.experimental.pallas import tpu as pltpu

F32 = jnp.float32
BF16 = jnp.bfloat16

HEAD_DIM = 64
RMS_EPS = 1e-6
RG_C = 8.0
CONV_W = 4

V7X_LANES = 128
V7X_SUBLANES = 8
V7X_VMEM_LIMIT_BYTES = 56 * 1024 * 1024
HEADS_PER_TILE = V7X_LANES // HEAD_DIM

EXP_ZERO_BELOW = -104.0
NEG_BIG = -1e30
LOG2E = 1.4426950408889634
FIXED_SHIFT_MAX_BOUND = 30.0

TOKEN_TILE = 512
FFN_TILE = 1024
TIME_TILE = 512
SCAN_CHUNK = 8
SB_TILE = 256
SB_SUBTILES = 2
FOX_TILE = 512
FOX_SUBTILES = 4
FOX_STRAIGHT_BLOCKS = 3


def _params(semantics):
    return pltpu.CompilerParams(dimension_semantics=semantics,
                                vmem_limit_bytes=V7X_VMEM_LIMIT_BYTES)


def _resident(shape):
    nd = len(shape)
    return pl.BlockSpec(shape, lambda *_: (0,) * nd, pipeline_mode=pl.Buffered(1))


def _rmsnorm_rows(x, g):
    ms = jnp.mean(x * x, axis=-1, keepdims=True)
    return x * lax.rsqrt(ms + RMS_EPS) * g


def _softplus(x):
    return jnp.maximum(x, 0.0) + jnp.log(1.0 + jnp.exp(-jnp.abs(x)))


def _sigmoid(x):
    return 1.0 / (1.0 + jnp.exp(-x))


def _log_sigmoid(x):
    return jnp.minimum(x, 0.0) - jnp.log(1.0 + jnp.exp2(jnp.abs(x) * (-LOG2E)))


def _ffn_kernel(*refs, d_ff, n_parts, sub_rows):
    x_ref, g_ref, win_ref, wout_ref = refs[:4]
    o_ref = refs[-1]
    for r in range(x_ref.shape[0] // sub_rows):
        rows = slice(r * sub_rows, (r + 1) * sub_rows)
        x = x_ref[rows, :]
        for i in range(n_parts):
            x = x + jnp.dot(refs[4 + 2 * i][rows, :], refs[5 + 2 * i][...], preferred_element_type=F32)
        h = _rmsnorm_rows(x, g_ref[...]).astype(BF16)
        ab = jnp.dot(h, win_ref[...], preferred_element_type=F32)
        a = ab[:, :d_ff]
        b = ab[:, d_ff:]
        t = (a * _sigmoid(a) * b).astype(BF16)
        y = jnp.dot(t, wout_ref[...], preferred_element_type=F32)
        o_ref[rows, :] = x + 0.5 * y


def _ffn(x2, g, w_in_all, w_out_all, which, parts=()):
    t, d = x2.shape
    d_ff = w_out_all.shape[2]
    tm = min(FFN_TILE, t)
    row = lambda i: (i, 0)
    pick = lambda i: (which[0], which[1], 0, 0)
    in_specs = [pl.BlockSpec((tm, d), row), _resident((1, d)),
                pl.BlockSpec((None, None) + w_in_all.shape[2:], pick, pipeline_mode=pl.Buffered(1)),
                pl.BlockSpec((None, None) + w_out_all.shape[2:], pick, pipeline_mode=pl.Buffered(1))]
    args = [x2, g.reshape(1, d), w_in_all, w_out_all]
    for y, w in parts:
        in_specs += [pl.BlockSpec((tm, y.shape[1]), row), _resident(w.shape)]
        args += [y, w]
    return pl.pallas_call(
        functools.partial(_ffn_kernel, d_ff=d_ff, n_parts=len(parts), sub_rows=min(TOKEN_TILE, tm)),
        out_shape=jax.ShapeDtypeStruct((t, d), F32),
        grid=(t // tm,),
        in_specs=in_specs,
        out_specs=pl.BlockSpec((tm, d), row),
        compiler_params=_params(("parallel",)),
        name="ffn",
    )(*args)


def _head_rmsnorm(t, gain_row, first_head):
    outs = []
    for j in range(t.shape[1] // V7X_LANES):
        tj = t[:, j * V7X_LANES:(j + 1) * V7X_LANES]
        sq = tj * tj
        s0 = jnp.sum(jnp.where(first_head, sq, 0.0), axis=-1, keepdims=True)
        s1 = jnp.sum(jnp.where(first_head, 0.0, sq), axis=-1, keepdims=True)
        ms = jnp.where(first_head, s0, s1) * (1.0 / HEAD_DIM)
        outs.append(tj * lax.rsqrt(ms + RMS_EPS) * gain_row)
    return jnp.concatenate(outs, axis=-1)


def _scan_linear(a, b, row):
    n = a.shape[0]
    s = 1
    while s < n:
        keep = row >= s
        b = jnp.where(keep, a * pltpu.roll(b, s, 0) + b, b)
        a = jnp.where(keep, a * pltpu.roll(a, s, 0), a)
        s *= 2
    return a, b


def _cumsum_rows(x, row):
    n = x.shape[0]
    s = 1
    while s < n:
        x = jnp.where(row >= s, x + pltpu.roll(x, s, 0), x)
        s *= 2
    return x


def _hymix_kernel(x_ref, g_ref, w_ref, qkg_ref, fb_ref, cw_ref, cb_ref, gw_ref, gb_ref, lam_ref,
                  q_ref, k_ref, v_ref, y_ref, c_ref, ct_ref, tail_ref, hc_ref, cc_ref,
                  *, d_rnn, fox_w, n_heads):
    @pl.when(pl.program_id(1) == 0)
    def _():
        tail_ref[...] = jnp.zeros_like(tail_ref)
        hc_ref[...] = jnp.zeros_like(hc_ref)
        cc_ref[...] = jnp.zeros_like(cc_ref)

    x = x_ref[0]
    tt = x.shape[0]
    h = _rmsnorm_rows(x, g_ref[...]).astype(BF16)
    p = jnp.dot(h, w_ref[...], preferred_element_type=F32)
    o = 2 * d_rnn
    first_head = lax.broadcasted_iota(jnp.int32, (tt, V7X_LANES), 1) < HEAD_DIM
    scale = HEAD_DIM ** -0.5
    q = _head_rmsnorm(p[:, o:o + fox_w], qkg_ref[0:1, :], first_head)
    q_ref[0] = (q * scale).astype(BF16)
    k = _head_rmsnorm(p[:, o + fox_w:o + 2 * fox_w], qkg_ref[1:2, :], first_head)
    k_ref[0] = k.astype(BF16)
    v_ref[0] = p[:, o + 2 * fox_w:o + 3 * fox_w].astype(BF16)
    log_f = _log_sigmoid(p[:, o + 3 * fox_w:] + fb_ref[...])

    u = p[:, :d_rnn]
    gate = p[:, d_rnn:o]

    tail = tail_ref[...]
    head_row = lax.broadcasted_iota(jnp.int32, (V7X_SUBLANES, d_rnn), 0)
    xc = cb_ref[...] + cw_ref[CONV_W - 1:CONV_W, :] * u
    for s in range(1, CONV_W):
        r = pltpu.roll(u, s, 0)
        head = jnp.where(head_row < s, pltpu.roll(tail, s, 0), r[:V7X_SUBLANES])
        shifted = jnp.concatenate([head, r[V7X_SUBLANES:]], axis=0)
        xc = xc + cw_ref[CONV_W - 1 - s:CONV_W - s, :] * shifted
    tail_ref[...] = u[tt - V7X_SUBLANES:, :]

    g = jnp.dot(xc.astype(BF16), gw_ref[...], preferred_element_type=F32) + gb_ref[...]
    r_gate = jax.nn.sigmoid(g[:, :d_rnn])
    i_gate = jax.nn.sigmoid(g[:, d_rnn:])
    log_a = -RG_C * r_gate * _softplus(-lam_ref[...])
    a = jnp.exp(log_a)
    one_minus_a2 = -jnp.tanh(log_a) * (a * a + 1.0)
    b = jnp.sqrt(one_minus_a2) * (i_gate * xc)

    chunk = min(SCAN_CHUNK, tt)
    chunk_row = lax.broadcasted_iota(jnp.int32, (chunk, d_rnn), 0)
    h_in = hc_ref[...]
    pieces = []
    for c in range(tt // chunk):
        big_a, big_b = _scan_linear(a[c * chunk:(c + 1) * chunk], b[c * chunk:(c + 1) * chunk], chunk_row)
        piece = big_a * h_in + big_b
        h_in = piece[chunk - 1:chunk, :]
        pieces.append(piece)
    hseq = jnp.concatenate(pieces, axis=0)
    hc_ref[...] = h_in
    y_ref[0] = (hseq * jax.nn.gelu(gate)).astype(BF16)

    lrow = lax.broadcasted_iota(jnp.int32, (tt, V7X_LANES), 0)
    c = _cumsum_rows(log_f, lrow) + cc_ref[...]
    cc_ref[...] = c[tt - 1:tt, :]
    c_ref[0] = c
    ct_ref[0] = c.T[:n_heads, :]


def _hymix(x3, g, w_pad, qk_gain, fb_pad, conv_w, conv_b, gate_w_dense, gate_b, lam, n_heads):
    bsz, s, d = x3.shape
    d_rnn = conv_w.shape[1]
    fox_w = n_heads * HEAD_DIM
    tt = min(TIME_TILE, s)
    blk = lambda b, i: (b, i, 0)
    return pl.pallas_call(
        functools.partial(_hymix_kernel, d_rnn=d_rnn, fox_w=fox_w, n_heads=n_heads),
        out_shape=(jax.ShapeDtypeStruct((bsz, s, fox_w), BF16),
                   jax.ShapeDtypeStruct((bsz, s, fox_w), BF16),
                   jax.ShapeDtypeStruct((bsz, s, fox_w), BF16),
                   jax.ShapeDtypeStruct((bsz, s, d_rnn), BF16),
                   jax.ShapeDtypeStruct((bsz, s, V7X_LANES), F32),
                   jax.ShapeDtypeStruct((bsz, n_heads, s), F32)),
        grid=(bsz, s // tt),
        in_specs=[pl.BlockSpec((1, tt, d), blk),
                  _resident((1, d)),
                  _resident(w_pad.shape),
                  _resident(qk_gain.shape),
                  _resident(fb_pad.shape),
                  _resident(conv_w.shape),
                  _resident((1, d_rnn)),
                  _resident(gate_w_dense.shape),
                  _resident((1, 2 * d_rnn)),
                  _resident((1, d_rnn))],
        out_specs=(pl.BlockSpec((1, tt, fox_w), blk),
                   pl.BlockSpec((1, tt, fox_w), blk),
                   pl.BlockSpec((1, tt, fox_w), blk),
                   pl.BlockSpec((1, tt, d_rnn), blk),
                   pl.BlockSpec((1, tt, V7X_LANES), blk),
                   pl.BlockSpec((1, n_heads, tt), lambda b, i: (b, 0, i))),
        scratch_shapes=[pltpu.VMEM((V7X_SUBLANES, d_rnn), F32),
                        pltpu.VMEM((1, d_rnn), F32),
                        pltpu.VMEM((1, V7X_LANES), F32)],
        compiler_params=_params(("parallel", "arbitrary")),
        name="hymix",
    )(x3, g.reshape(1, d), w_pad, qk_gain, fb_pad, conv_w, conv_b.reshape(1, d_rnn), gate_w_dense,
      gate_b.reshape(1, 2 * d_rnn), lam.reshape(1, d_rnn))


def _qk(qm, kj):
    return lax.dot_general(qm, kj, (((1,), (1,)), ((), ())), preferred_element_type=F32)


def _max_key_norms(k_ref, kmax_ref, chunk):
    s = k_ref.shape[1]
    gi = lax.broadcasted_iota(jnp.int32, (V7X_LANES, V7X_LANES), 0) < HEAD_DIM
    gj = lax.broadcasted_iota(jnp.int32, (V7X_LANES, V7X_LANES), 1) < HEAD_DIM
    same_head = jnp.where(gi == gj, 1.0, 0.0).astype(BF16)
    m = jnp.zeros((1, V7X_LANES), F32)
    for c in range(s // chunk):
        kk = k_ref[0, c * chunk:(c + 1) * chunk, :].astype(F32)
        sq = jnp.dot((kk * kk).astype(BF16), same_head, preferred_element_type=F32)
        m = jnp.maximum(m, jnp.max(sq, axis=0, keepdims=True))
    kmax_ref[...] = jnp.sqrt(m)


def _logit_bound(qm, kmax_row, head_mask_row):
    qf = qm.astype(F32)
    qn = jnp.sqrt(jnp.sum(qf * qf, axis=-1, keepdims=True))
    km = jnp.max(jnp.where(head_mask_row, kmax_row, 0.0), axis=-1, keepdims=True)
    return qn * km * 1.01 + 1e-3


def _head_masks(rows):
    lane = lax.broadcasted_iota(jnp.int32, (rows, V7X_LANES), 1)
    lane_row = lax.broadcasted_iota(jnp.int32, (1, V7X_LANES), 1)
    full = [(lane >= hd * HEAD_DIM) & (lane < (hd + 1) * HEAD_DIM) for hd in range(HEADS_PER_TILE)]
    one = [(lane_row >= hd * HEAD_DIM) & (lane_row < (hd + 1) * HEAD_DIM)
           for hd in range(HEADS_PER_TILE)]
    return lane, full, one


def _sb_kernel(q_ref, k_ref, v_ref, o_ref, kmax_ref, *, tile):
    qi = pl.program_id(2)
    n_sub = q_ref.shape[1] // tile
    tq = tk = tile

    @pl.when(qi == 0)
    def _():
        _max_key_norms(k_ref, kmax_ref, min(1024, k_ref.shape[1]))

    _, head_masks, head_rows = _head_masks(tq)
    rows = lax.broadcasted_iota(jnp.int32, (tq, tk), 0)
    cols = lax.broadcasted_iota(jnp.int32, (tq, tk), 1)
    strictly_past = cols < rows
    tri = jnp.where(rows >= cols, 1.0, 0.0).astype(BF16)
    kmax_row = kmax_ref[...]

    def block(qneg, j, run, acc, diag):
        start = pl.multiple_of(j * tk, tk)
        kj = k_ref[0, pl.ds(start, tk), :]
        vj = v_ref[0, pl.ds(start, tk), :]
        nz = _qk(qneg, kj)
        if diag:
            nz = jnp.where(strictly_past, nz, -NEG_BIG)
        log_nb = _log_sigmoid(nz)
        incl = jnp.dot(log_nb.astype(BF16), tri, preferred_element_type=F32)
        w = jnp.exp(incl - nz if run is None else incl + run - nz)
        acc = acc + jnp.dot(w.astype(BF16), vj, preferred_element_type=F32)
        total = incl[:, 0:1]
        return (total if run is None else run + total), acc

    state = []
    for sub in range(n_sub):
        blk = qi * n_sub + sub
        q = q_ref[0, sub * tile:(sub + 1) * tile, :]
        left = jnp.maximum(blk - 1, 0)
        left_offset = jnp.where(blk >= 1, 0.0, NEG_BIG)
        for hd in range(HEADS_PER_TILE):
            qm = jnp.where(head_masks[hd], -q, jnp.zeros_like(q))
            bound = _logit_bound(qm, kmax_row, head_rows[hd])
            run, acc = block(qm, blk, None, jnp.zeros((tq, V7X_LANES), F32), True)
            run, acc = block(qm, left, run + left_offset, acc, False)
            state.append((blk, qm, bound, run, acc, jnp.max(run + bound) > EXP_ZERO_BELOW))

    results = []
    for blk, qm, bound, run, acc, go in state:

        def cond(carry):
            return jnp.logical_and(carry[0] >= 0, carry[1])

        def body(carry, qm=qm, bound=bound):
            j, _, run, acc = carry
            run, acc = block(qm, j, run, acc, False)
            return j - 1, jnp.max(run + bound) > EXP_ZERO_BELOW, run, acc

        results.append(lax.while_loop(cond, body, (blk - 2, go, run, acc))[3])

    for sub in range(n_sub):
        res = results[sub * HEADS_PER_TILE]
        for hd in range(1, HEADS_PER_TILE):
            res = jnp.where(head_masks[hd], results[sub * HEADS_PER_TILE + hd], res)
        o_ref[0, sub * tile:(sub + 1) * tile, :] = res.astype(BF16)


def _sb_attention(q, k, v):
    bsz, s, w = q.shape
    tile = min(SB_TILE, s)
    tq = min(SB_TILE * SB_SUBTILES, s)
    n_tiles = w // V7X_LANES
    return pl.pallas_call(
        functools.partial(_sb_kernel, tile=tile),
        out_shape=jax.ShapeDtypeStruct((bsz, s, w), BF16),
        grid=(bsz, n_tiles, s // tq),
        in_specs=[pl.BlockSpec((1, tq, V7X_LANES), lambda b, p, i: (b, i, p)),
                  pl.BlockSpec((1, s, V7X_LANES), lambda b, p, i: (b, 0, p)),
                  pl.BlockSpec((1, s, V7X_LANES), lambda b, p, i: (b, 0, p))],
        out_specs=pl.BlockSpec((1, tq, V7X_LANES), lambda b, p, i: (b, i, p)),
        scratch_shapes=[pltpu.VMEM((1, V7X_LANES), F32)],
        compiler_params=_params(("parallel", "parallel", "arbitrary")),
        name="sb_attn",
    )(q, k, v)


def _fox_kernel(q_ref, k_ref, v_ref, c_ref, ct_ref, o_ref, kmax_ref, *, tile):
    lane_tile = pl.program_id(1)
    qi = pl.program_id(2)
    n_sub = q_ref.shape[1] // tile
    tq = tk = tile

    @pl.when(qi == 0)
    def _():
        _max_key_norms(k_ref, kmax_ref, min(1024, k_ref.shape[1]))

    lane, head_masks, head_rows = _head_masks(tq)
    _, key_masks, _ = _head_masks(tk)
    rows = lax.broadcasted_iota(jnp.int32, (tq, tk), 0)
    cols = lax.broadcasted_iota(jnp.int32, (tq, tk), 1)
    causal = cols <= rows
    kmax_row = kmax_ref[...]

    def load(head, j):
        start = pl.multiple_of(j * tk, tk)
        kj = k_ref[0, pl.ds(start, tk), :]
        vj = v_ref[0, pl.ds(start, tk), :]
        ck = ct_ref[0, pl.ds(head, 1), pl.ds(start, tk)]
        return kj, vj, ck

    units = []
    for sub in range(n_sub):
        q = q_ref[0, sub * tile:(sub + 1) * tile, :]
        c_blk = c_ref[0, sub * tile:(sub + 1) * tile, :]
        for hd in range(HEADS_PER_TILE):
            head = lane_tile * HEADS_PER_TILE + hd
            qm = jnp.where(head_masks[hd], q, jnp.zeros_like(q))
            bound = _logit_bound(qm, kmax_row, head_rows[hd])
            cq = jnp.sum(jnp.where(lane == head, c_blk, 0.0), axis=-1, keepdims=True)
            units.append((qi * n_sub + sub, hd, head, qm, bound, cq))

    def merge(outs):
        tiles = []
        for sub in range(n_sub):
            res = outs[sub * HEADS_PER_TILE]
            for hd in range(1, HEADS_PER_TILE):
                res = jnp.where(head_masks[hd], outs[sub * HEADS_PER_TILE + hd], res)
            tiles.append(res)
        return jnp.concatenate(tiles, axis=0)

    def fixed_shift():
        def block(hd, head, qm, shift, cq_top, j, acc, diag):
            kj, vj, ck = load(head, j)
            s = _qk(qm, kj) + shift - ck
            if diag:
                s = jnp.where(causal, s, NEG_BIG)
            p = jnp.exp(s)
            v1 = jnp.where(key_masks[hd], vj, jnp.ones_like(vj))
            acc = acc + jnp.dot(p.astype(BF16), v1, preferred_element_type=F32)
            return acc, cq_top - jnp.max(ck) > EXP_ZERO_BELOW

        state = []
        for blk, hd, head, qm, bound, cq in units:
            shift = cq - bound
            cq_top = jnp.max(cq)
            acc, go = block(hd, head, qm, shift, cq_top, blk, jnp.zeros((tq, V7X_LANES), F32), True)
            for back in range(1, FOX_STRAIGHT_BLOCKS):
                offset = jnp.where(blk >= back, 0.0, NEG_BIG)
                acc, go = block(hd, head, qm, shift + offset, cq_top, jnp.maximum(blk - back, 0), acc, False)
            state.append((shift, cq_top, acc, go))

        outs = []
        for (blk, hd, head, qm, _, _), (shift, cq_top, acc, go) in zip(units, state):

            def cond(carry):
                return jnp.logical_and(carry[0] >= 0, carry[1])

            def body(carry, hd=hd, head=head, qm=qm, shift=shift, cq_top=cq_top):
                j, _, acc = carry
                acc, go = block(hd, head, qm, shift, cq_top, j, acc, False)
                return j - 1, go, acc

            _, _, acc = lax.while_loop(cond, body, (blk - FOX_STRAIGHT_BLOCKS, go, acc))
            outs.append(acc / pltpu.roll(acc, HEAD_DIM, 1))
        return merge(outs)

    def running_shift():
        outs = []
        for blk, _, head, qm, bound, cq in units:

            def block(j, m, l, acc, diag, head=head, qm=qm, bound=bound, cq=cq):
                kj, vj, ck = load(head, j)
                s = _qk(qm, kj) + cq - ck
                if diag:
                    s = jnp.where(causal, s, NEG_BIG)
                m_new = jnp.maximum(m, jnp.max(s, axis=-1, keepdims=True))
                alpha = jnp.exp(m - m_new)
                p = jnp.exp(s - m_new)
                l = alpha * l + jnp.sum(p, axis=-1, keepdims=True)
                acc = alpha * acc + jnp.dot(p.astype(BF16), vj, preferred_element_type=F32)
                reach = cq - jnp.max(ck, axis=-1, keepdims=True)
                return m_new, l, acc, jnp.max(bound + reach - m_new) > EXP_ZERO_BELOW

            m, l, acc, go = block(blk, jnp.full((tq, 1), NEG_BIG, F32), jnp.zeros((tq, 1), F32),
                                  jnp.zeros((tq, V7X_LANES), F32), True)

            def cond(carry):
                return jnp.logical_and(carry[0] >= 0, carry[1])

            def body(carry, block=block):
                j, _, m, l, acc = carry
                m, l, acc, go = block(j, m, l, acc, False)
                return j - 1, go, m, l, acc

            _, _, _, l, acc = lax.while_loop(cond, body, (blk - 1, go, m, l, acc))
            outs.append(acc / l)
        return merge(outs)

    largest_bound = functools.reduce(jnp.maximum, [jnp.max(u[4]) for u in units])
    res = lax.cond(largest_bound <= FIXED_SHIFT_MAX_BOUND, fixed_shift, running_shift)
    o_ref[0] = res.astype(BF16)


def _fox_attention(q, k, v, c, ct):
    bsz, s, w = q.shape
    n_heads = ct.shape[1]
    tile = min(FOX_TILE, s)
    tq = min(FOX_TILE * FOX_SUBTILES, s)
    n_tiles = w // V7X_LANES
    return pl.pallas_call(
        functools.partial(_fox_kernel, tile=tile),
        out_shape=jax.ShapeDtypeStruct((bsz, s, w), BF16),
        grid=(bsz, n_tiles, s // tq),
        in_specs=[pl.BlockSpec((1, tq, V7X_LANES), lambda b, p, i: (b, i, p)),
                  pl.BlockSpec((1, s, V7X_LANES), lambda b, p, i: (b, 0, p)),
                  pl.BlockSpec((1, s, V7X_LANES), lambda b, p, i: (b, 0, p)),
                  pl.BlockSpec((1, tq, V7X_LANES), lambda b, p, i: (b, i, 0)),
                  pl.BlockSpec((1, n_heads, s), lambda b, p, i: (b, 0, 0))],
        out_specs=pl.BlockSpec((1, tq, V7X_LANES), lambda b, p, i: (b, i, p)),
        scratch_shapes=[pltpu.VMEM((1, V7X_LANES), F32)],
        compiler_params=_params(("parallel", "parallel", "arbitrary")),
        name="fox_attn",
    )(q, k, v, c, ct)


def _sbin_kernel(x_ref, g_ref, w_ref, q_ref, k_ref, v_ref, *, width):
    x = x_ref[...]
    h = _rmsnorm_rows(x, g_ref[...]).astype(BF16)
    p = jnp.dot(h, w_ref[...], preferred_element_type=F32)
    q_ref[...] = (p[:, :width] * (HEAD_DIM ** -0.5)).astype(BF16)
    k_ref[...] = p[:, width:2 * width].astype(BF16)
    v_ref[...] = p[:, 2 * width:].astype(BF16)


def _sbin(x2, g, w_qkv):
    t, d = x2.shape
    width = w_qkv.shape[1] // 3
    tm = min(TOKEN_TILE, t)
    row = lambda i: (i, 0)
    return pl.pallas_call(
        functools.partial(_sbin_kernel, width=width),
        out_shape=tuple(jax.ShapeDtypeStruct((t, width), BF16) for _ in range(3)),
        grid=(t // tm,),
        in_specs=[pl.BlockSpec((tm, d), row), _resident((1, d)), _resident(w_qkv.shape)],
        out_specs=tuple(pl.BlockSpec((tm, width), row) for _ in range(3)),
        compiler_params=_params(("parallel",)),
        name="sbin",
    )(x2, g.reshape(1, d), w_qkv)


def _block_diag_gates(gate_w):
    n_gates, n_blocks, r, _ = gate_w.shape
    eye = jnp.eye(n_blocks, dtype=gate_w.dtype)
    dense = jnp.einsum('ghij,hk->ghikj', gate_w, eye).reshape(n_gates, n_blocks * r, n_blocks * r)
    return jnp.concatenate([dense[g] for g in range(n_gates)], axis=1)


def kernel(x, ffn_norm, ffn_w_in, ffn_w_out, mix_norm, hy_w_in, rg_conv_w, rg_conv_b, rg_gate_w,
           rg_gate_b, rg_lambda, fox_fgate_b, fox_qk_norm, hy_w_out, sb_w_qkv, sb_w_out):
    bsz, s, d = x.shape
    depth = ffn_norm.shape[0]
    d_rnn = rg_conv_w.shape[2]
    n_fox = fox_fgate_b.shape[1]
    fox_w = n_fox * HEAD_DIM
    t = bsz * s

    x2 = x.reshape(t, d)
    w_in_all = ffn_w_in.astype(BF16)
    w_out_all = ffn_w_out.astype(BF16)
    for layer in range(depth):
        x2 = _ffn(x2, ffn_norm[layer, 0], w_in_all, w_out_all, (layer, 0))
        if layer % 2 == 0:
            e = layer // 2
            pad = V7X_LANES - n_fox
            w_pad = jnp.pad(hy_w_in[e], ((0, 0), (0, pad))).astype(BF16)
            fb_pad = jnp.pad(fox_fgate_b[e], (0, pad)).reshape(1, V7X_LANES)
            qk_gain = jnp.tile(fox_qk_norm[e], (1, HEADS_PER_TILE))
            q, k, v, y_rnn, c, ct = _hymix(x2.reshape(bsz, s, d), mix_norm[layer], w_pad, qk_gain, fb_pad,
                                           rg_conv_w[e], rg_conv_b[e],
                                           _block_diag_gates(rg_gate_w[e]).astype(BF16),
                                           rg_gate_b[e].reshape(-1), rg_lambda[e], n_fox)
            y_fox = _fox_attention(q, k, v, c, ct)
            w_out = hy_w_out[e].astype(BF16)
            mixer_out = [(y_rnn.reshape(t, d_rnn), w_out[:d_rnn]),
                         (y_fox.reshape(t, fox_w), w_out[d_rnn:])]
        else:
            o = layer // 2
            q, k, v = _sbin(x2, mix_norm[layer], sb_w_qkv[o].astype(BF16))
            width = q.shape[1]
            y = _sb_attention(q.reshape(bsz, s, width), k.reshape(bsz, s, width),
                              v.reshape(bsz, s, width))
            mixer_out = [(y.reshape(t, width), sb_w_out[o].astype(BF16))]
        x2 = _ffn(x2, ffn_norm[layer, 1], w_in_all, w_out_all, (layer, 1), mixer_out)
    return x2.reshape(bsz, s, d)
```

```python
import functools

import jax
import jax.numpy as jnp
from jax import lax
from jax.experimental import pallas as pl
from jax.experimental.pallas import tpu as pltpu

F32 = jnp.float32
BF16 = jnp.bfloat16

HEAD_DIM = 64
RMS_EPS = 1e-6
RG_C = 8.0
CONV_W = 4

V7X_LANES = 128
V7X_SUBLANES = 8
V7X_VMEM_LIMIT_BYTES = 56 * 1024 * 1024
HEADS_PER_TILE = V7X_LANES // HEAD_DIM

EXP_ZERO_BELOW = -104.0
NEG_BIG = -1e30
LOG2E = 1.4426950408889634
FIXED_SHIFT_MAX_BOUND = 30.0

TOKEN_TILE = 512
FFN_TILE = 1024
TIME_TILE = 512
SCAN_CHUNK = 8
SB_TILE = 256
SB_SUBTILES = 2
FOX_TILE = 512
FOX_SUBTILES = 4
FOX_STRAIGHT_BLOCKS = 3


def _params(semantics):
    return pltpu.CompilerParams(dimension_semantics=semantics,
                                vmem_limit_bytes=V7X_VMEM_LIMIT_BYTES)


def _resident(shape):
    nd = len(shape)
    return pl.BlockSpec(shape, lambda *_: (0,) * nd, pipeline_mode=pl.Buffered(1))


def _rmsnorm_rows(x, g):
    ms = jnp.mean(x * x, axis=-1, keepdims=True)
    return x * lax.rsqrt(ms + RMS_EPS) * g


def _softplus(x):
    return jnp.maximum(x, 0.0) + jnp.log(1.0 + jnp.exp(-jnp.abs(x)))


def _sigmoid(x):
    return 1.0 / (1.0 + jnp.exp(-x))


def _log_sigmoid(x):
    return jnp.minimum(x, 0.0) - jnp.log(1.0 + jnp.exp2(jnp.abs(x) * (-LOG2E)))


def _ffn_kernel(*refs, d_ff, n_parts, sub_rows):
    x_ref, g_ref, win_ref, wout_ref = refs[:4]
    o_ref = refs[-1]
    for r in range(x_ref.shape[0] // sub_rows):
        rows = slice(r * sub_rows, (r + 1) * sub_rows)
        x = x_ref[rows, :]
        for i in range(n_parts):
            x = x + jnp.dot(refs[4 + 2 * i][rows, :], refs[5 + 2 * i][...], preferred_element_type=F32)
        h = _rmsnorm_rows(x, g_ref[...]).astype(BF16)
        ab = jnp.dot(h, win_ref[...], preferred_element_type=F32)
        a = ab[:, :d_ff]
        b = ab[:, d_ff:]
        t = ((0.5 * a) * (1.0 + jnp.tanh(0.5 * a)) * b).astype(BF16)
        y = jnp.dot(t, wout_ref[...], preferred_element_type=F32)
        o_ref[rows, :] = x + 0.5 * y


def _ffn(x2, g, w_in_all, w_out_all, which, parts=()):
    t, d = x2.shape
    d_ff = w_out_all.shape[2]
    tm = min(FFN_TILE, t)
    row = lambda i: (i, 0)
    pick = lambda i: (which[0], which[1], 0, 0)
    in_specs = [pl.BlockSpec((tm, d), row), _resident((1, d)),
                pl.BlockSpec((None, None) + w_in_all.shape[2:], pick, pipeline_mode=pl.Buffered(1)),
                pl.BlockSpec((None, None) + w_out_all.shape[2:], pick, pipeline_mode=pl.Buffered(1))]
    args = [x2, g.reshape(1, d), w_in_all, w_out_all]
    for y, w in parts:
        in_specs += [pl.BlockSpec((tm, y.shape[1]), row), _resident(w.shape)]
        args += [y, w]
    return pl.pallas_call(
        functools.partial(_ffn_kernel, d_ff=d_ff, n_parts=len(parts), sub_rows=min(TOKEN_TILE, tm)),
        out_shape=jax.ShapeDtypeStruct((t, d), F32),
        grid=(t // tm,),
        in_specs=in_specs,
        out_specs=pl.BlockSpec((tm, d), row),
        compiler_params=_params(("parallel",)),
        name="ffn",
    )(*args)


def _head_rmsnorm(t, gain_row, first_head):
    outs = []
    for j in range(t.shape[1] // V7X_LANES):
        tj = t[:, j * V7X_LANES:(j + 1) * V7X_LANES]
        sq = tj * tj
        s0 = jnp.sum(jnp.where(first_head, sq, 0.0), axis=-1, keepdims=True)
        s1 = jnp.sum(jnp.where(first_head, 0.0, sq), axis=-1, keepdims=True)
        ms = jnp.where(first_head, s0, s1) * (1.0 / HEAD_DIM)
        outs.append(tj * lax.rsqrt(ms + RMS_EPS) * gain_row)
    return jnp.concatenate(outs, axis=-1)


def _scan_linear(a, b, row):
    n = a.shape[0]
    s = 1
    while s < n:
        keep = row >= s
        b = jnp.where(keep, a * pltpu.roll(b, s, 0) + b, b)
        a = jnp.where(keep, a * pltpu.roll(a, s, 0), a)
        s *= 2
    return a, b


def _cumsum_rows(x, row):
    n = x.shape[0]
    s = 1
    while s < n:
        x = jnp.where(row >= s, x + pltpu.roll(x, s, 0), x)
        s *= 2
    return x


def _hymix_kernel(x_ref, g_ref, w_ref, qkg_ref, fb_ref, cw_ref, cb_ref, gw_ref, gb_ref, lam_ref,
                  q_ref, k_ref, v_ref, y_ref, c_ref, ct_ref, tail_ref, hc_ref, cc_ref,
                  *, d_rnn, fox_w, n_heads):
    @pl.when(pl.program_id(1) == 0)
    def _():
        tail_ref[...] = jnp.zeros_like(tail_ref)
        hc_ref[...] = jnp.zeros_like(hc_ref)
        cc_ref[...] = jnp.zeros_like(cc_ref)

    x = x_ref[0]
    tt = x.shape[0]
    h = _rmsnorm_rows(x, g_ref[...]).astype(BF16)
    p = jnp.dot(h, w_ref[...], preferred_element_type=F32)
    o = 2 * d_rnn
    first_head = lax.broadcasted_iota(jnp.int32, (tt, V7X_LANES), 1) < HEAD_DIM
    scale = HEAD_DIM ** -0.5
    q = _head_rmsnorm(p[:, o:o + fox_w], qkg_ref[0:1, :], first_head)
    q_ref[0] = (q * scale).astype(BF16)
    k = _head_rmsnorm(p[:, o + fox_w:o + 2 * fox_w], qkg_ref[1:2, :], first_head)
    k_ref[0] = k.astype(BF16)
    v_ref[0] = p[:, o + 2 * fox_w:o + 3 * fox_w].astype(BF16)
    log_f = _log_sigmoid(p[:, o + 3 * fox_w:] + fb_ref[...])

    u = p[:, :d_rnn]
    gate = p[:, d_rnn:o]

    tail = tail_ref[...]
    head_row = lax.broadcasted_iota(jnp.int32, (V7X_SUBLANES, d_rnn), 0)
    xc = cb_ref[...] + cw_ref[CONV_W - 1:CONV_W, :] * u
    for s in range(1, CONV_W):
        r = pltpu.roll(u, s, 0)
        head = jnp.where(head_row < s, pltpu.roll(tail, s, 0), r[:V7X_SUBLANES])
        shifted = jnp.concatenate([head, r[V7X_SUBLANES:]], axis=0)
        xc = xc + cw_ref[CONV_W - 1 - s:CONV_W - s, :] * shifted
    tail_ref[...] = u[tt - V7X_SUBLANES:, :]

    g = jnp.dot(xc.astype(BF16), gw_ref[...], preferred_element_type=F32) + gb_ref[...]
    r_gate = jax.nn.sigmoid(g[:, :d_rnn])
    i_gate = jax.nn.sigmoid(g[:, d_rnn:])
    log_a = -RG_C * r_gate * _softplus(-lam_ref[...])
    a = jnp.exp(log_a)
    one_minus_a2 = -jnp.tanh(log_a) * (a * a + 1.0)
    b = jnp.sqrt(one_minus_a2) * (i_gate * xc)

    chunk = min(SCAN_CHUNK, tt)
    chunk_row = lax.broadcasted_iota(jnp.int32, (chunk, d_rnn), 0)
    h_in = hc_ref[...]
    pieces = []
    for c in range(tt // chunk):
        big_a, big_b = _scan_linear(a[c * chunk:(c + 1) * chunk], b[c * chunk:(c + 1) * chunk], chunk_row)
        piece = big_a * h_in + big_b
        h_in = piece[chunk - 1:chunk, :]
        pieces.append(piece)
    hseq = jnp.concatenate(pieces, axis=0)
    hc_ref[...] = h_in
    y_ref[0] = (hseq * jax.nn.gelu(gate)).astype(BF16)

    lrow = lax.broadcasted_iota(jnp.int32, (tt, V7X_LANES), 0)
    c = _cumsum_rows(log_f, lrow) + cc_ref[...]
    cc_ref[...] = c[tt - 1:tt, :]
    c_ref[0] = c
    ct_ref[0] = c.T[:n_heads, :]


def _hymix(x3, g, w_pad, qk_gain, fb_pad, conv_w, conv_b, gate_w_dense, gate_b, lam, n_heads):
    bsz, s, d = x3.shape
    d_rnn = conv_w.shape[1]
    fox_w = n_heads * HEAD_DIM
    tt = min(TIME_TILE, s)
    blk = lambda b, i: (b, i, 0)
    return pl.pallas_call(
        functools.partial(_hymix_kernel, d_rnn=d_rnn, fox_w=fox_w, n_heads=n_heads),
        out_shape=(jax.ShapeDtypeStruct((bsz, s, fox_w), BF16),
                   jax.ShapeDtypeStruct((bsz, s, fox_w), BF16),
                   jax.ShapeDtypeStruct((bsz, s, fox_w), BF16),
                   jax.ShapeDtypeStruct((bsz, s, d_rnn), BF16),
                   jax.ShapeDtypeStruct((bsz, s, V7X_LANES), F32),
                   jax.ShapeDtypeStruct((bsz, n_heads, s), F32)),
        grid=(bsz, s // tt),
        in_specs=[pl.BlockSpec((1, tt, d), blk),
                  _resident((1, d)),
                  _resident(w_pad.shape),
                  _resident(qk_gain.shape),
                  _resident(fb_pad.shape),
                  _resident(conv_w.shape),
                  _resident((1, d_rnn)),
                  _resident(gate_w_dense.shape),
                  _resident((1, 2 * d_rnn)),
                  _resident((1, d_rnn))],
        out_specs=(pl.BlockSpec((1, tt, fox_w), blk),
                   pl.BlockSpec((1, tt, fox_w), blk),
                   pl.BlockSpec((1, tt, fox_w), blk),
                   pl.BlockSpec((1, tt, d_rnn), blk),
                   pl.BlockSpec((1, tt, V7X_LANES), blk),
                   pl.BlockSpec((1, n_heads, tt), lambda b, i: (b, 0, i))),
        scratch_shapes=[pltpu.VMEM((V7X_SUBLANES, d_rnn), F32),
                        pltpu.VMEM((1, d_rnn), F32),
                        pltpu.VMEM((1, V7X_LANES), F32)],
        compiler_params=_params(("parallel", "arbitrary")),
        name="hymix",
    )(x3, g.reshape(1, d), w_pad, qk_gain, fb_pad, conv_w, conv_b.reshape(1, d_rnn), gate_w_dense,
      gate_b.reshape(1, 2 * d_rnn), lam.reshape(1, d_rnn))


def _qk(qm, kj):
    return lax.dot_general(qm, kj, (((1,), (1,)), ((), ())), preferred_element_type=F32)


def _max_key_norms(k_ref, kmax_ref, chunk):
    s = k_ref.shape[1]
    gi = lax.broadcasted_iota(jnp.int32, (V7X_LANES, V7X_LANES), 0) < HEAD_DIM
    gj = lax.broadcasted_iota(jnp.int32, (V7X_LANES, V7X_LANES), 1) < HEAD_DIM
    same_head = jnp.where(gi == gj, 1.0, 0.0).astype(BF16)
    m = jnp.zeros((1, V7X_LANES), F32)
    for c in range(s // chunk):
        kk = k_ref[0, c * chunk:(c + 1) * chunk, :].astype(F32)
        sq = jnp.dot((kk * kk).astype(BF16), same_head, preferred_element_type=F32)
        m = jnp.maximum(m, jnp.max(sq, axis=0, keepdims=True))
    kmax_ref[...] = jnp.sqrt(m)


def _logit_bound(qm, kmax_row, head_mask_row):
    qf = qm.astype(F32)
    qn = jnp.sqrt(jnp.sum(qf * qf, axis=-1, keepdims=True))
    km = jnp.max(jnp.where(head_mask_row, kmax_row, 0.0), axis=-1, keepdims=True)
    return qn * km * 1.01 + 1e-3


def _head_masks(rows):
    lane = lax.broadcasted_iota(jnp.int32, (rows, V7X_LANES), 1)
    lane_row = lax.broadcasted_iota(jnp.int32, (1, V7X_LANES), 1)
    full = [(lane >= hd * HEAD_DIM) & (lane < (hd + 1) * HEAD_DIM) for hd in range(HEADS_PER_TILE)]
    one = [(lane_row >= hd * HEAD_DIM) & (lane_row < (hd + 1) * HEAD_DIM)
           for hd in range(HEADS_PER_TILE)]
    return lane, full, one


def _sb_kernel(q_ref, k_ref, v_ref, o_ref, kmax_ref, *, tile):
    qi = pl.program_id(2)
    n_sub = q_ref.shape[1] // tile
    tq = tk = tile

    @pl.when(qi == 0)
    def _():
        _max_key_norms(k_ref, kmax_ref, min(1024, k_ref.shape[1]))

    _, head_masks, head_rows = _head_masks(tq)
    rows = lax.broadcasted_iota(jnp.int32, (tq, tk), 0)
    cols = lax.broadcasted_iota(jnp.int32, (tq, tk), 1)
    strictly_past = cols < rows
    tri = jnp.where(rows >= cols, 1.0, 0.0).astype(BF16)
    kmax_row = kmax_ref[...]

    def block(qneg, j, run, acc, diag):
        start = pl.multiple_of(j * tk, tk)
        kj = k_ref[0, pl.ds(start, tk), :]
        vj = v_ref[0, pl.ds(start, tk), :]
        nz = _qk(qneg, kj)
        if diag:
            nz = jnp.where(strictly_past, nz, -NEG_BIG)
        log_nb = _log_sigmoid(nz)
        incl = jnp.dot(log_nb.astype(BF16), tri, preferred_element_type=F32)
        w = jnp.exp(incl - nz if run is None else incl + run - nz)
        acc = acc + jnp.dot(w.astype(BF16), vj, preferred_element_type=F32)
        total = incl[:, 0:1]
        return (total if run is None else run + total), acc

    state = []
    for sub in range(n_sub):
        blk = qi * n_sub + sub
        q = q_ref[0, sub * tile:(sub + 1) * tile, :]
        left = jnp.maximum(blk - 1, 0)
        left_offset = jnp.where(blk >= 1, 0.0, NEG_BIG)
        for hd in range(HEADS_PER_TILE):
            qm = jnp.where(head_masks[hd], -q, jnp.zeros_like(q))
            bound = _logit_bound(qm, kmax_row, head_rows[hd])
            run, acc = block(qm, blk, None, jnp.zeros((tq, V7X_LANES), F32), True)
            run, acc = block(qm, left, run + left_offset, acc, False)
            state.append((blk, qm, bound, run, acc, jnp.max(run + bound) > EXP_ZERO_BELOW))

    results = []
    for blk, qm, bound, run, acc, go in state:

        def cond(carry):
            return jnp.logical_and(carry[0] >= 0, carry[1])

        def body(carry, qm=qm, bound=bound):
            j, _, run, acc = carry
            run, acc = block(qm, j, run, acc, False)
            return j - 1, jnp.max(run + bound) > EXP_ZERO_BELOW, run, acc

        results.append(lax.while_loop(cond, body, (blk - 2, go, run, acc))[3])

    for sub in range(n_sub):
        res = results[sub * HEADS_PER_TILE]
        for hd in range(1, HEADS_PER_TILE):
            res = jnp.where(head_masks[hd], results[sub * HEADS_PER_TILE + hd], res)
        o_ref[0, sub * tile:(sub + 1) * tile, :] = res.astype(BF16)


def _sb_attention(q, k, v):
    bsz, s, w = q.shape
    tile = min(SB_TILE, s)
    tq = min(SB_TILE * SB_SUBTILES, s)
    n_tiles = w // V7X_LANES
    return pl.pallas_call(
        functools.partial(_sb_kernel, tile=tile),
        out_shape=jax.ShapeDtypeStruct((bsz, s, w), BF16),
        grid=(bsz, n_tiles, s // tq),
        in_specs=[pl.BlockSpec((1, tq, V7X_LANES), lambda b, p, i: (b, i, p)),
                  pl.BlockSpec((1, s, V7X_LANES), lambda b, p, i: (b, 0, p)),
                  pl.BlockSpec((1, s, V7X_LANES), lambda b, p, i: (b, 0, p))],
        out_specs=pl.BlockSpec((1, tq, V7X_LANES), lambda b, p, i: (b, i, p)),
        scratch_shapes=[pltpu.VMEM((1, V7X_LANES), F32)],
        compiler_params=_params(("parallel", "parallel", "arbitrary")),
        name="sb_attn",
    )(q, k, v)


def _fox_kernel(q_ref, k_ref, v_ref, c_ref, ct_ref, o_ref, kmax_ref, *, tile):
    lane_tile = pl.program_id(1)
    qi = pl.program_id(2)
    n_sub = q_ref.shape[1] // tile
    tq = tk = tile

    @pl.when(qi == 0)
    def _():
        _max_key_norms(k_ref, kmax_ref, min(1024, k_ref.shape[1]))

    lane, head_masks, head_rows = _head_masks(tq)
    _, key_masks, _ = _head_masks(tk)
    rows = lax.broadcasted_iota(jnp.int32, (tq, tk), 0)
    cols = lax.broadcasted_iota(jnp.int32, (tq, tk), 1)
    causal = cols <= rows
    kmax_row = kmax_ref[...]

    def load(head, j):
        start = pl.multiple_of(j * tk, tk)
        kj = k_ref[0, pl.ds(start, tk), :]
        vj = v_ref[0, pl.ds(start, tk), :]
        ck = ct_ref[0, pl.ds(head, 1), pl.ds(start, tk)]
        return kj, vj, ck

    units = []
    for sub in range(n_sub):
        q = q_ref[0, sub * tile:(sub + 1) * tile, :]
        c_blk = c_ref[0, sub * tile:(sub + 1) * tile, :]
        for hd in range(HEADS_PER_TILE):
            head = lane_tile * HEADS_PER_TILE + hd
            qm = jnp.where(head_masks[hd], q, jnp.zeros_like(q))
            bound = _logit_bound(qm, kmax_row, head_rows[hd])
            cq = jnp.sum(jnp.where(lane == head, c_blk, 0.0), axis=-1, keepdims=True)
            units.append((qi * n_sub + sub, hd, head, qm, bound, cq))

    def merge(outs):
        tiles = []
        for sub in range(n_sub):
            res = outs[sub * HEADS_PER_TILE]
            for hd in range(1, HEADS_PER_TILE):
                res = jnp.where(head_masks[hd], outs[sub * HEADS_PER_TILE + hd], res)
            tiles.append(res)
        return jnp.concatenate(tiles, axis=0)

    def fixed_shift():
        def block(hd, head, qm, shift, cq_top, j, acc, diag):
            kj, vj, ck = load(head, j)
            s = _qk(qm, kj) + shift - ck
            if diag:
                s = jnp.where(causal, s, NEG_BIG)
            p = jnp.exp(s)
            v1 = jnp.where(key_masks[hd], vj, jnp.ones_like(vj))
            acc = acc + jnp.dot(p.astype(BF16), v1, preferred_element_type=F32)
            return acc, cq_top - jnp.max(ck) > EXP_ZERO_BELOW

        state = []
        for blk, hd, head, qm, bound, cq in units:
            shift = cq - bound
            cq_top = jnp.max(cq)
            acc, go = block(hd, head, qm, shift, cq_top, blk, jnp.zeros((tq, V7X_LANES), F32), True)
            for back in range(1, FOX_STRAIGHT_BLOCKS):
                offset = jnp.where(blk >= back, 0.0, NEG_BIG)
                acc, go = block(hd, head, qm, shift + offset, cq_top, jnp.maximum(blk - back, 0), acc, False)
            state.append((shift, cq_top, acc, go))

        outs = []
        for (blk, hd, head, qm, _, _), (shift, cq_top, acc, go) in zip(units, state):

            def cond(carry):
                return jnp.logical_and(carry[0] >= 0, carry[1])

            def body(carry, hd=hd, head=head, qm=qm, shift=shift, cq_top=cq_top):
                j, _, acc = carry
                acc, go = block(hd, head, qm, shift, cq_top, j, acc, False)
                return j - 1, go, acc

            _, _, acc = lax.while_loop(cond, body, (blk - FOX_STRAIGHT_BLOCKS, go, acc))
            outs.append(acc / pltpu.roll(acc, HEAD_DIM, 1))
        return merge(outs)

    def running_shift():
        outs = []
        for blk, _, head, qm, bound, cq in units:

            def block(j, m, l, acc, diag, head=head, qm=qm, bound=bound, cq=cq):
                kj, vj, ck = load(head, j)
                s = _qk(qm, kj) + cq - ck
                if diag:
                    s = jnp.where(causal, s, NEG_BIG)
                m_new = jnp.maximum(m, jnp.max(s, axis=-1, keepdims=True))
                alpha = jnp.exp(m - m_new)
                p = jnp.exp(s - m_new)
                l = alpha * l + jnp.sum(p, axis=-1, keepdims=True)
                acc = alpha * acc + jnp.dot(p.astype(BF16), vj, preferred_element_type=F32)
                reach = cq - jnp.max(ck, axis=-1, keepdims=True)
                return m_new, l, acc, jnp.max(bound + reach - m_new) > EXP_ZERO_BELOW

            m, l, acc, go = block(blk, jnp.full((tq, 1), NEG_BIG, F32), jnp.zeros((tq, 1), F32),
                                  jnp.zeros((tq, V7X_LANES), F32), True)

            def cond(carry):
                return jnp.logical_and(carry[0] >= 0, carry[1])

            def body(carry, block=block):
                j, _, m, l, acc = carry
                m, l, acc, go = block(j, m, l, acc, False)
                return j - 1, go, m, l, acc

            _, _, _, l, acc = lax.while_loop(cond, body, (blk - 1, go, m, l, acc))
            outs.append(acc / l)
        return merge(outs)

    largest_bound = functools.reduce(jnp.maximum, [jnp.max(u[4]) for u in units])
    res = lax.cond(largest_bound <= FIXED_SHIFT_MAX_BOUND, fixed_shift, running_shift)
    o_ref[0] = res.astype(BF16)


def _fox_attention(q, k, v, c, ct):
    bsz, s, w = q.shape
    n_heads = ct.shape[1]
    tile = min(FOX_TILE, s)
    tq = min(FOX_TILE * FOX_SUBTILES, s)
    n_tiles = w // V7X_LANES
    return pl.pallas_call(
        functools.partial(_fox_kernel, tile=tile),
        out_shape=jax.ShapeDtypeStruct((bsz, s, w), BF16),
        grid=(bsz, n_tiles, s // tq),
        in_specs=[pl.BlockSpec((1, tq, V7X_LANES), lambda b, p, i: (b, i, p)),
                  pl.BlockSpec((1, s, V7X_LANES), lambda b, p, i: (b, 0, p)),
                  pl.BlockSpec((1, s, V7X_LANES), lambda b, p, i: (b, 0, p)),
                  pl.BlockSpec((1, tq, V7X_LANES), lambda b, p, i: (b, i, 0)),
                  pl.BlockSpec((1, n_heads, s), lambda b, p, i: (b, 0, 0))],
        out_specs=pl.BlockSpec((1, tq, V7X_LANES), lambda b, p, i: (b, i, p)),
        scratch_shapes=[pltpu.VMEM((1, V7X_LANES), F32)],
        compiler_params=_params(("parallel", "parallel", "arbitrary")),
        name="fox_attn",
    )(q, k, v, c, ct)


def _sbin_kernel(x_ref, g_ref, w_ref, q_ref, k_ref, v_ref, *, width):
    x = x_ref[...]
    h = _rmsnorm_rows(x, g_ref[...]).astype(BF16)
    p = jnp.dot(h, w_ref[...], preferred_element_type=F32)
    q_ref[...] = (p[:, :width] * (HEAD_DIM ** -0.5)).astype(BF16)
    k_ref[...] = p[:, width:2 * width].astype(BF16)
    v_ref[...] = p[:, 2 * width:].astype(BF16)


def _sbin(x2, g, w_qkv):
    t, d = x2.shape
    width = w_qkv.shape[1] // 3
    tm = min(TOKEN_TILE, t)
    row = lambda i: (i, 0)
    return pl.pallas_call(
        functools.partial(_sbin_kernel, width=width),
        out_shape=tuple(jax.ShapeDtypeStruct((t, width), BF16) for _ in range(3)),
        grid=(t // tm,),
        in_specs=[pl.BlockSpec((tm, d), row), _resident((1, d)), _resident(w_qkv.shape)],
        out_specs=tuple(pl.BlockSpec((tm, width), row) for _ in range(3)),
        compiler_params=_params(("parallel",)),
        name="sbin",
    )(x2, g.reshape(1, d), w_qkv)


def _block_diag_gates(gate_w):
    n_gates, n_blocks, r, _ = gate_w.shape
    eye = jnp.eye(n_blocks, dtype=gate_w.dtype)
    dense = jnp.einsum('ghij,hk->ghikj', gate_w, eye).reshape(n_gates, n_blocks * r, n_blocks * r)
    return jnp.concatenate([dense[g] for g in range(n_gates)], axis=1)


def kernel(x, ffn_norm, ffn_w_in, ffn_w_out, mix_norm, hy_w_in, rg_conv_w, rg_conv_b, rg_gate_w,
           rg_gate_b, rg_lambda, fox_fgate_b, fox_qk_norm, hy_w_out, sb_w_qkv, sb_w_out):
    bsz, s, d = x.shape
    depth = ffn_norm.shape[0]
    d_rnn = rg_conv_w.shape[2]
    n_fox = fox_fgate_b.shape[1]
    fox_w = n_fox * HEAD_DIM
    t = bsz * s

    x2 = x.reshape(t, d)
    w_in_all = ffn_w_in.astype(BF16)
    w_out_all = ffn_w_out.astype(BF16)
    for layer in range(depth):
        x2 = _ffn(x2, ffn_norm[layer, 0], w_in_all, w_out_all, (layer, 0))
        if layer % 2 == 0:
            e = layer // 2
            pad = V7X_LANES - n_fox
            w_pad = jnp.pad(hy_w_in[e], ((0, 0), (0, pad))).astype(BF16)
            fb_pad = jnp.pad(fox_fgate_b[e], (0, pad)).reshape(1, V7X_LANES)
            qk_gain = jnp.tile(fox_qk_norm[e], (1, HEADS_PER_TILE))
            q, k, v, y_rnn, c, ct = _hymix(x2.reshape(bsz, s, d), mix_norm[layer], w_pad, qk_gain, fb_pad,
                                           rg_conv_w[e], rg_conv_b[e],
                                           _block_diag_gates(rg_gate_w[e]).astype(BF16),
                                           rg_gate_b[e].reshape(-1), rg_lambda[e], n_fox)
            y_fox = _fox_attention(q, k, v, c, ct)
            w_out = hy_w_out[e].astype(BF16)
            mixer_out = [(y_rnn.reshape(t, d_rnn), w_out[:d_rnn]),
                         (y_fox.reshape(t, fox_w), w_out[d_rnn:])]
        else:
            o = layer // 2
            q, k, v = _sbin(x2, mix_norm[layer], sb_w_qkv[o].astype(BF16))
            width = q.shape[1]
            y = _sb_attention(q.reshape(bsz, s, width), k.reshape(bsz, s, width),
                              v.reshape(bsz, s, width))
            mixer_out = [(y.reshape(t, width), sb_w_out[o].astype(BF16))]
        x2 = _ffn(x2, ffn_norm[layer, 1], w_in_all, w_out_all, (layer, 1), mixer_out)
    return x2.reshape(bsz, s, d)
```
